```python
import functools
import jax, jax.numpy as jnp
from jax import lax
import numpy as np

D_MODEL = 1024
BATCH = 2
SEQ = 8192
DEPTH = 2
DEC_BATCH = 32
DEC_SEQ = 8
PAST_LEN = 16384
PAGE_SIZE = 128

D_MIX = D_MODEL
HEAD_DIM = 64
NSA_WIDTH = D_MIX // 2
NSA_HEADS = NSA_WIDTH // HEAD_DIM
NSA_KV = 2
NSA_G = NSA_HEADS // NSA_KV
CMP_BLOCK = 64
SEL_BLOCK = 64
TOP_K = 16
WINDOW = 512
Q_BLOCK = 128
GLA_WIDTH = D_MIX // 4
GLA_HEADS = 4
GLA_DV = GLA_WIDTH // GLA_HEADS
GLA_DK = GLA_DV // 2
GLA_RANK = 16
GLA_TAU = 16.0
GLA_CHUNK = 64
CONV_WIDTH = D_MIX - NSA_WIDTH - GLA_WIDTH
CONV_K = 3
EPS = 1e-6
NEG = -1e30
FORCE = 1e4

PROJ_LAYOUT = (
    ('nsa_q', NSA_WIDTH),
    ('nsa_cmp', 2 * NSA_KV * HEAD_DIM),
    ('nsa_slc', 2 * NSA_KV * HEAD_DIM),
    ('nsa_win', 2 * NSA_KV * HEAD_DIM),
    ('nsa_gate', 3 * NSA_HEADS),
    ('nsa_z', NSA_WIDTH),
    ('gla_q', GLA_HEADS * GLA_DK),
    ('gla_k', GLA_HEADS * GLA_DK),
    ('gla_v', GLA_WIDTH),
    ('gla_a', GLA_RANK),
    ('gla_z', GLA_WIDTH),
    ('conv_b', CONV_WIDTH),
    ('conv_c', CONV_WIDTH),
    ('conv_h', CONV_WIDTH),
    ('conv_z', CONV_WIDTH),
)
PROJ_COLS = sum(w for _, w in PROJ_LAYOUT)

kernel_name = 'hymba_nsa_gla_shortconv_step'


def rmsnorm(x, g):
    xf = x.astype(jnp.float32)
    xf = xf * lax.rsqrt(jnp.mean(xf * xf, axis=-1, keepdims=True) + EPS)
    return xf.astype(x.dtype) * g


def masked_softmax(s, mask):
    s = jnp.where(mask, s.astype(jnp.float32), NEG)
    return jax.nn.softmax(s, axis=-1) * mask


def in_proj(h, w_in):
    p = jnp.einsum('btd,dc->btc', h, w_in)
    out, off = {}, 0
    for name, width in PROJ_LAYOUT:
        out[name] = p[..., off:off + width]
        off += width
    return out


def compress_blocks(kv, w_ck, w_cv):
    b_, t_ = kv.shape[:2]
    kb = kv.reshape(b_, t_ // CMP_BLOCK, CMP_BLOCK, 2, NSA_KV, HEAD_DIM)
    w = jnp.stack([w_ck, w_cv], axis=-1)
    return jnp.einsum('bnlchd,lc->bnchd', kb, w)


def nsa_core(q, qpos, gates, kv_cmp, gather_sel, kv_band, band_pos, n_blocks):
    dt = q.dtype
    qs = q * (HEAD_DIM ** -0.5)
    nc = kv_cmp.shape[1]
    cmp_end = (jnp.arange(nc) + 1) * CMP_BLOCK - 1
    m_c = (cmp_end[None, :] <= qpos[:, None])[None, :, None, None, :]
    p_c = masked_softmax(jnp.einsum('bqhgd,bchd->bqhgc', qs, kv_cmp[:, :, 0]), m_c)
    o_c = jnp.einsum('bqhgc,bchd->bqhgd', p_c.astype(dt), kv_cmp[:, :, 1])
    imp = jnp.pad(p_c.sum(axis=3), ((0, 0), (0, 0), (0, 0), (0, n_blocks - nc)))
    blk = jnp.arange(n_blocks)[None, :]
    cur = (qpos // SEL_BLOCK)[:, None]
    forced = ((blk == 0) | (blk == cur) | (blk == cur - 1))[None, :, None, :]
    visible = (blk <= cur)[None, :, None, :]
    score = jnp.where(forced, FORCE, jnp.where(visible, imp, -1.0))
    _, idx = lax.top_k(score, min(TOP_K, n_blocks))
    b_, q_, h_, j_ = idx.shape
    kv_sel = gather_sel(idx).reshape(b_, q_, h_, j_ * SEL_BLOCK, 2, HEAD_DIM)
    tok = (idx[..., None] * SEL_BLOCK + jnp.arange(SEL_BLOCK)).reshape(b_, q_, h_, j_ * SEL_BLOCK)
    m_s = (tok <= qpos[None, :, None, None])[:, :, :, None, :]
    p_s = masked_softmax(jnp.einsum('bqhgd,bqhkd->bqhgk', qs, kv_sel[..., 0, :]), m_s)
    o_s = jnp.einsum('bqhgk,bqhkd->bqhgd', p_s.astype(dt), kv_sel[..., 1, :])
    dpos = qpos[:, None] - band_pos[None, :]
    m_w = ((dpos >= 0) & (dpos < WINDOW) & (band_pos[None, :] >= 0))[None, :, None, None, :]
    p_w = masked_softmax(jnp.einsum('bqhgd,bkhd->bqhgk', qs, kv_band[:, :, 0]), m_w)
    o_w = jnp.einsum('bqhgk,bkhd->bqhgd', p_w.astype(dt), kv_band[:, :, 1])
    return gates[..., 0:1] * o_c + gates[..., 1:2] * o_s + gates[..., 2:3] * o_w


def nsa_prompt(q, gates, kv_c, kv_s, kv_w, w_ck, w_cv):
    b_, s_ = q.shape[:2]
    nb = s_ // SEL_BLOCK
    kv_cmp = compress_blocks(kv_c[:, :(s_ // CMP_BLOCK) * CMP_BLOCK], w_ck, w_cv)
    kb = kv_s.reshape(b_, nb, SEL_BLOCK, 2, NSA_KV, HEAD_DIM)
    bi = jnp.arange(b_)[:, None, None, None]
    gi = jnp.arange(NSA_KV)[None, None, :, None]

    def gather_sel(idx):
        return kb[bi, idx, :, :, gi]

    kvw_pad = jnp.pad(kv_w, ((0, 0), (WINDOW, 0), (0, 0), (0, 0), (0, 0)))
    qb = min(Q_BLOCK, s_)

    def block(i):
        q0 = i * qb
        qpos = q0 + jnp.arange(qb)
        band = lax.dynamic_slice_in_dim(kvw_pad, q0, qb + WINDOW, axis=1)
        band_pos = q0 - WINDOW + jnp.arange(qb + WINDOW)
        return nsa_core(lax.dynamic_slice_in_dim(q, q0, qb, axis=1), qpos,
                        lax.dynamic_slice_in_dim(gates, q0, qb, axis=1),
                        kv_cmp, gather_sel, band, band_pos, nb)

    o = lax.map(block, jnp.arange(s_ // qb))
    o = jnp.moveaxis(o, 0, 1).reshape(b_, s_, NSA_KV, NSA_G, HEAD_DIM)
    w_keep = min(WINDOW, s_)
    return o, (kv_c, kv_s, kv_w[:, s_ - w_keep:])


def nsa_sample(q, gates, kv_c, kv_s, kv_w, l, w_ck, w_cv, cache_cmp_kv, cache_slc_kv, win_buf, page_table):
    bd, t_new = q.shape[:2]
    past = cache_cmp_kv[l, page_table].reshape(bd, PAST_LEN, 2, NSA_KV, HEAD_DIM)
    total = PAST_LEN + t_new
    nc = total // CMP_BLOCK
    full = jnp.concatenate([past, kv_c], axis=1)[:, :nc * CMP_BLOCK]
    kv_cmp = compress_blocks(full, w_ck, w_cv)
    n_past_blk = PAST_LEN // SEL_BLOCK
    n_new_blk = -(-t_new // SEL_BLOCK)
    nb = n_past_blk + n_new_blk
    sub = PAGE_SIZE // SEL_BLOCK
    new_blk = jnp.pad(kv_s, ((0, 0), (0, n_new_blk * SEL_BLOCK - t_new), (0, 0), (0, 0), (0, 0)))
    new_blk = new_blk.reshape(bd, n_new_blk, SEL_BLOCK, 2, NSA_KV, HEAD_DIM)
    bi = jnp.arange(bd)[:, None, None, None]
    gi = jnp.arange(NSA_KV)[None, None, :, None]

    def gather_sel(idx):
        jp = jnp.minimum(idx, n_past_blk - 1)
        phys = page_table[bi, jp // sub]
        rows = (jp % sub)[..., None] * SEL_BLOCK + jnp.arange(SEL_BLOCK)
        from_pool = cache_slc_kv[l, phys[..., None], rows, :, gi[..., None]]
        jn = jnp.clip(idx - n_past_blk, 0, n_new_blk - 1)
        from_new = new_blk[bi, jn, :, :, gi]
        return jnp.where((idx < n_past_blk)[..., None, None, None], from_pool, from_new)

    w_buf = win_buf.shape[1]
    band = jnp.concatenate([win_buf, kv_w], axis=1)
    band_pos = PAST_LEN - w_buf + jnp.arange(w_buf + t_new)
    qpos = PAST_LEN + jnp.arange(t_new)
    o = nsa_core(q, qpos, gates, kv_cmp, gather_sel, band, band_pos, nb)
    return o, (kv_c, kv_s, band[:, band.shape[1] - w_buf:])


def gla_chunked(q, k, v, log_a, s0):
    b_, t_ = q.shape[:2]
    c = min(GLA_CHUNK, t_)
    n = -(-t_ // c)
    pad = n * c - t_

    def to_chunks(a):
        a = jnp.pad(a, ((0, 0), (0, pad), (0, 0), (0, 0)))
        return a.reshape(b_, n, c, a.shape[2], a.shape[3]).transpose(1, 0, 3, 2, 4)

    causal = jnp.tril(jnp.ones((c, c), dtype=bool))[:, :, None]

    def step(s, xs):
        qi, ki, vi, ai = (a.astype(jnp.float32) for a in xs)
        bcum = jnp.cumsum(ai, axis=2)
        inter = jnp.einsum('bhtk,bhkv->bhtv', qi * jnp.exp(bcum), s)
        diff = bcum[:, :, :, None, :] - bcum[:, :, None, :, :]
        decay = jnp.exp(jnp.where(causal, diff, -jnp.inf))
        att = jnp.einsum('bhtk,bhsk,bhtsk->bhts', qi, ki, decay)
        o = inter + jnp.einsum('bhts,bhsv->bhtv', att, vi)
        blast = bcum[:, :, -1:, :]
        s = jnp.exp(blast[:, :, 0, :])[..., None] * s + jnp.einsum('bhsk,bhsv->bhkv', ki * jnp.exp(blast - bcum), vi)
        return s, o

    s, o = lax.scan(step, s0.astype(jnp.float32), (to_chunks(q), to_chunks(k), to_chunks(v), to_chunks(log_a)))
    o = o.transpose(1, 0, 3, 2, 4).reshape(b_, n * c, q.shape[2], v.shape[3])[:, :t_]
    return o.astype(v.dtype), s


def short_conv(u, buf, w):
    t_ = u.shape[1]
    up = jnp.concatenate([buf.astype(u.dtype), u], axis=1)
    y = sum(up[:, i:i + t_] * w[i] for i in range(CONV_K))
    return y, up[:, up.shape[1] - (CONV_K - 1):]


def layer_forward(x, l, W, nsa_fn, gla_s0, conv_buf):
    b_, t_ = x.shape[:2]
    p = in_proj(rmsnorm(x, W['norm_pre'][l]), W['w_in'][l])
    q = p['nsa_q'].reshape(b_, t_, NSA_KV, NSA_G, HEAD_DIM)
    gates = jax.nn.sigmoid(p['nsa_gate'].reshape(b_, t_, NSA_KV, NSA_G, 3))
    kv_c = p['nsa_cmp'].reshape(b_, t_, 2, NSA_KV, HEAD_DIM)
    kv_s = p['nsa_slc'].reshape(b_, t_, 2, NSA_KV, HEAD_DIM)
    kv_w = p['nsa_win'].reshape(b_, t_, 2, NSA_KV, HEAD_DIM)
    o_nsa, (new_c, new_s, new_w) = nsa_fn(q, gates, kv_c, kv_s, kv_w)
    o_nsa = o_nsa.reshape(b_, t_, NSA_WIDTH) * jax.nn.silu(p['nsa_z'])
    gq = p['gla_q'].reshape(b_, t_, GLA_HEADS, GLA_DK) * (GLA_DK ** -0.5)
    gk = p['gla_k'].reshape(b_, t_, GLA_HEADS, GLA_DK)
    gv = p['gla_v'].reshape(b_, t_, GLA_HEADS, GLA_DV)
    a_logit = jnp.einsum('btr,rk->btk', p['gla_a'], W['gla_a2'][l]) + W['gla_ab'][l]
    log_a = (jax.nn.log_sigmoid(a_logit.astype(jnp.float32)) / GLA_TAU).reshape(b_, t_, GLA_HEADS, GLA_DK)
    o_gla, s_gla = gla_chunked(gq, gk, gv, log_a, gla_s0)
    o_gla = rmsnorm(o_gla, W['gla_norm'][l]).reshape(b_, t_, GLA_WIDTH) * jax.nn.silu(p['gla_z'])
    y_conv, new_buf = short_conv(p['conv_c'] * p['conv_h'], conv_buf, W['conv_w'][l])
    o_conv = p['conv_b'] * y_conv * jax.nn.silu(p['conv_z'])
    mix = jnp.concatenate([o_nsa, o_gla, o_conv], axis=-1)
    out = jnp.einsum('btc,cd->btd', mix, W['w_out'][l])
    x = x + rmsnorm(out, W['norm_post'][l])
    return x, (new_c, new_s, new_w, s_gla.astype(x.dtype), new_buf)


def setup_inputs(seed: int = 0) -> dict:
    key = jax.random.key(seed)
    ks = jax.random.split(key, 20)
    nrm = jax.random.normal
    n_pages = PAST_LEN // PAGE_SIZE
    n_pool = (DEC_BATCH * n_pages * 5) // 4
    w_buf = min(WINDOW, PAST_LEN)
    kv_tail = (2, NSA_KV, HEAD_DIM)
    page_table = jax.random.permutation(ks[7], n_pool)[:DEC_BATCH * n_pages]
    return {
        'x_prompt': nrm(ks[0], (BATCH, SEQ, D_MODEL), jnp.float32),
        'x_sample': nrm(ks[1], (DEC_BATCH, DEC_SEQ, D_MODEL), jnp.float32),
        'cache_cmp_kv': nrm(ks[2], (DEPTH, n_pool, PAGE_SIZE) + kv_tail, jnp.float32),
        'cache_slc_kv': nrm(ks[3], (DEPTH, n_pool, PAGE_SIZE) + kv_tail, jnp.float32),
        'cache_win_kv': nrm(ks[4], (DEPTH, DEC_BATCH, w_buf) + kv_tail, jnp.float32),
        'state_gla': 0.5 * nrm(ks[5], (DEPTH, DEC_BATCH, GLA_HEADS, GLA_DK, GLA_DV), jnp.float32),
        'state_conv': nrm(ks[6], (DEPTH, DEC_BATCH, CONV_K - 1, CONV_WIDTH), jnp.float32),
        'page_table': page_table.reshape(DEC_BATCH, n_pages).astype(jnp.int32),
        'norm_pre': 1.0 + 0.1 * nrm(ks[8], (DEPTH, D_MODEL), jnp.float32),
        'norm_post': 1.0 + 0.1 * nrm(ks[9], (DEPTH, D_MODEL), jnp.float32),
        'w_in': nrm(ks[10], (DEPTH, D_MODEL, PROJ_COLS), jnp.float32) * D_MODEL ** -0.5,
        'w_out': nrm(ks[11], (DEPTH, D_MIX, D_MODEL), jnp.float32) * D_MIX ** -0.5,
        'w_cmp_k': (1.0 + 0.1 * nrm(ks[12], (DEPTH, CMP_BLOCK), jnp.float32)) / CMP_BLOCK,
        'w_cmp_v': (1.0 + 0.1 * nrm(ks[13], (DEPTH, CMP_BLOCK), jnp.float32)) / CMP_BLOCK,
        'gla_a2': nrm(ks[14], (DEPTH, GLA_RANK, GLA_HEADS * GLA_DK), jnp.float32) * GLA_RANK ** -0.5,
        'gla_ab': 0.1 * nrm(ks[15], (DEPTH, GLA_HEADS * GLA_DK), jnp.float32),
        'gla_norm': 1.0 + 0.1 * nrm(ks[16], (DEPTH, GLA_DV), jnp.float32),
        'conv_w': nrm(ks[17], (DEPTH, CONV_K, CONV_WIDTH), jnp.float32) * CONV_K ** -0.5,
    }


def reference(x_prompt, x_sample, cache_cmp_kv, cache_slc_kv, cache_win_kv, state_gla, state_conv, page_table,
              norm_pre, norm_post, w_in, w_out, w_cmp_k, w_cmp_v, gla_a2, gla_ab, gla_norm, conv_w):
    W = {'norm_pre': norm_pre, 'norm_post': norm_post, 'w_in': w_in, 'w_out': w_out,
         'gla_a2': gla_a2, 'gla_ab': gla_ab, 'gla_norm': gla_norm, 'conv_w': conv_w}
    xp, xs = x_prompt, x_sample
    st_p, st_s = [], []
    for l in range(DEPTH):
        nsa_p = functools.partial(nsa_prompt, w_ck=w_cmp_k[l], w_cv=w_cmp_v[l])
        gla0 = jnp.zeros((xp.shape[0], GLA_HEADS, GLA_DK, GLA_DV), jnp.float32)
        conv0 = jnp.zeros((xp.shape[0], CONV_K - 1, CONV_WIDTH), xp.dtype)
        xp, sp = layer_forward(xp, l, W, nsa_p, gla0, conv0)
        nsa_s = functools.partial(nsa_sample, l=l, w_ck=w_cmp_k[l], w_cv=w_cmp_v[l],
                                  cache_cmp_kv=cache_cmp_kv, cache_slc_kv=cache_slc_kv,
                                  win_buf=cache_win_kv[l], page_table=page_table)
        xs, ss = layer_forward(xs, l, W, nsa_s, state_gla[l], state_conv[l])
        st_p.append(sp)
        st_s.append(ss)
    new_cmp_p = jnp.stack([s[0] for s in st_p])
    new_cmp_s = jnp.stack([s[0] for s in st_s])
    new_slc_p = jnp.stack([s[1] for s in st_p])
    new_slc_s = jnp.stack([s[1] for s in st_s])
    new_win_p = jnp.stack([s[2] for s in st_p])
    new_win_s = jnp.stack([s[2] for s in st_s])
    new_gla_p = jnp.stack([s[3] for s in st_p])
    new_gla_s = jnp.stack([s[3] for s in st_s])
    new_conv_p = jnp.stack([s[4] for s in st_p])
    new_conv_s = jnp.stack([s[4] for s in st_s])
    return (xp, xs, new_cmp_p, new_cmp_s, new_slc_p, new_slc_s, new_win_p, new_win_s,
            new_gla_p, new_gla_s, new_conv_p, new_conv_s)
```

```python
import functools

import jax
import jax.numpy as jnp
from jax import lax
from jax.experimental import pallas as pl
from jax.experimental.pallas import tpu as pltpu

F32 = jnp.float32
BF = jnp.bfloat16

HEAD_DIM = 64
NSA_KV = 2
NSA_G = 4
NSA_HEADS = NSA_KV * NSA_G
NSA_WIDTH = NSA_HEADS * HEAD_DIM
KVW = 2 * NSA_KV * HEAD_DIM
HALF = NSA_KV * HEAD_DIM
CMP_BLOCK = 64
SEL_BLOCK = 64
TOP_K = 16
WINDOW = 512
GLA_HEADS = 4
GLA_DK = 32
GLA_DV = 64
GLA_KW = GLA_HEADS * GLA_DK
GLA_WIDTH = GLA_HEADS * GLA_DV
GLA_RANK = 16
GLA_TAU = 16.0
GLA_CHUNK = 64
GLA_SUB = 16
CONV_WIDTH = 256
CONV_K = 3
N_GATE = 3 * NSA_HEADS
EPS = 1e-6
NEG = -1e30

LANES = 128
VMEM_LIMIT = 48 * 1024 * 1024

C_Q = 0
C_CMP = C_Q + NSA_WIDTH
C_SLC = C_CMP + KVW
C_WIN = C_SLC + KVW
C_NZ = C_WIN + KVW
C_GLA = C_NZ + NSA_WIDTH
W_GLA = 2 * GLA_KW + 2 * GLA_WIDTH
C_CONV = C_GLA + W_GLA
W_CONV = 4 * CONV_WIDTH
C_MISC = C_CONV + W_CONV
PROJ_PAD = C_MISC + LANES

PROJ_TM = 512
NSA_TQ = 128
PAGES_PER_STEP = 16


def _cparams(n_axes, vmem=VMEM_LIMIT):
    return pltpu.CompilerParams(dimension_semantics=("arbitrary",) * n_axes, vmem_limit_bytes=vmem)


def _dot(a, b):
    return jnp.dot(a, b, preferred_element_type=F32)


def _dot_nt(a, b):
    return lax.dot_general(a, b, (((1,), (1,)), ((), ())), preferred_element_type=F32)


def _split3(a):
    a0 = a.astype(BF)
    r = a - a0.astype(F32)
    a1 = r.astype(BF)
    a2 = (r - a1.astype(F32)).astype(BF)
    return a0, a1, a2


def _transpose_mxu(y, dtype):
    n = y.shape[1]
    eye = jnp.where(lax.broadcasted_iota(jnp.int32, (n, n), 0) == lax.broadcasted_iota(jnp.int32, (n, n), 1),
                    1.0, 0.0).astype(BF)
    if dtype == BF:
        return _dot_nt(eye, y.astype(BF))
    return sum(_dot_nt(eye, part) for part in _split3(y))


def _silu(x):
    return x * (1.0 / (1.0 + jnp.exp(-x)))


def _sigmoid(x):
    return 1.0 / (1.0 + jnp.exp(-x))


def _proj_body(x_ref, g_ref, w_ref, wc_ref, *outs, prompt):
    x = x_ref[...]
    h = x * lax.rsqrt(jnp.mean(x * x, axis=-1, keepdims=True) + EPS) * g_ref[...]
    hb = h.astype(BF)

    def proj(c0, n):
        return _dot(hb, w_ref[:, c0:c0 + n])

    q_ref, kvc_ref, kvs_ref, kvw_ref, nz_ref, gla_ref, conv_ref, misc_ref = outs[:8]
    q_ref[...] = proj(C_Q, NSA_WIDTH)
    kvc = proj(C_CMP, KVW)
    kvs = proj(C_SLC, KVW)
    kvw = proj(C_WIN, KVW)
    kvc_ref[...] = kvc
    kvs_ref[...] = kvs
    kvw_ref[...] = kvw
    nz_ref[...] = proj(C_NZ, NSA_WIDTH)
    gla_ref[...] = proj(C_GLA, W_GLA)
    conv_ref[...] = proj(C_CONV, W_CONV)
    misc_ref[...] = proj(C_MISC, LANES)
    if prompt:
        ksb_ref, kwb_ref, vst_ref, vwt_ref, cmp_ref = outs[8:]
        tm = x.shape[0]
        ksb_ref[...] = kvs[:, :HALF].astype(BF)
        kwb_ref[...] = kvw[:, :HALF].astype(BF)
        vst_ref[0, 0] = kvs[:, HALF:].T.astype(BF)
        for j in range(tm // LANES):
            vwt_ref[0, j] = kvw[j * LANES:(j + 1) * LANES, HALF:].T.astype(BF)
        cmp_ref[...] = jnp.sum(kvc.reshape(tm // CMP_BLOCK, CMP_BLOCK, KVW) * wc_ref[...][None], axis=1)


def _proj(x2, g, w, wc, *, batch, seq, prompt):
    n, d = x2.shape
    tm = PROJ_TM if prompt else n
    steps = n // tm
    per_seq = seq // tm if prompt else 1
    row = lambda i: (i, 0)
    fixed = lambda i: (0, 0)
    widths = [NSA_WIDTH, KVW, KVW, KVW, NSA_WIDTH, W_GLA, W_CONV, LANES]
    out_shape = [jax.ShapeDtypeStruct((n, c), F32) for c in widths]
    out_specs = [pl.BlockSpec((tm, c), row) for c in widths]
    if prompt:
        out_shape += [jax.ShapeDtypeStruct((n, HALF), BF), jax.ShapeDtypeStruct((n, HALF), BF),
                      jax.ShapeDtypeStruct((batch, seq // tm, HALF, tm), BF),
                      jax.ShapeDtypeStruct((batch, seq // LANES, HALF, LANES), BF),
                      jax.ShapeDtypeStruct((n // CMP_BLOCK, KVW), F32)]
        out_specs += [pl.BlockSpec((tm, HALF), row), pl.BlockSpec((tm, HALF), row),
                      pl.BlockSpec((1, 1, HALF, tm), lambda i: (i // per_seq, i % per_seq, 0, 0)),
                      pl.BlockSpec((1, tm // LANES, HALF, LANES), lambda i: (i // per_seq, i % per_seq, 0, 0)),
                      pl.BlockSpec((tm // CMP_BLOCK, KVW), row)]
    return pl.pallas_call(
        functools.partial(_proj_body, prompt=prompt),
        grid=(steps,),
        in_specs=[pl.BlockSpec((tm, d), row), pl.BlockSpec((1, d), fixed),
                  pl.BlockSpec((d, PROJ_PAD), fixed), pl.BlockSpec((CMP_BLOCK, KVW), fixed)],
        out_specs=out_specs,
        out_shape=out_shape,
        compiler_params=_cparams(1),
        name="proj_prompt" if prompt else "proj_sample",
    )(x2, g, w, wc)


def _select_blocks(imp, forced, visible, n_pick):
    nb = imp.shape[0]
    blk = lax.broadcasted_iota(jnp.int32, imp.shape, 0)
    score = jnp.where(forced, -jnp.inf, jnp.where(visible, imp, -1.0))
    sel = forced
    for _ in range(n_pick):
        m = jnp.max(score, axis=0, keepdims=True)
        first = jnp.min(jnp.where(score == m, blk, nb), axis=0, keepdims=True)
        hit = blk == first
        sel = sel | hit
        score = jnp.where(hit, -jnp.inf, score)
    return sel


def _nsa_prompt_body(q_ref, misc_ref, nz_ref, cmp_ref, ks_ref, vst_ref, kw_ref, vwt_ref, out_ref, bias_ref,
                     *, seq):
    tq = NSA_TQ
    tk = PROJ_TM
    nb = seq // SEL_BLOCK
    wl = NSA_G * tq
    i = pl.program_id(1)
    q0 = i * tq
    qf = q_ref[...] * (HEAD_DIM ** -0.5)
    gate_t = _sigmoid(misc_ref[...]).T
    lane_head = lax.broadcasted_iota(jnp.int32, (tq, LANES), 1) // HEAD_DIM
    kc = cmp_ref[0]
    kcb = kc[:, :HALF].astype(BF)
    vct = kc[:, HALF:].T.astype(BF)

    qpos = q0 + (lax.broadcasted_iota(jnp.int32, (1, wl), 1) % tq)
    blk_w = lax.broadcasted_iota(jnp.int32, (nb, wl), 0)
    vis_c = blk_w * CMP_BLOCK + (CMP_BLOCK - 1) <= qpos
    blk1 = lax.broadcasted_iota(jnp.int32, (nb, tq), 0)
    cur = (q0 + lax.broadcasted_iota(jnp.int32, (nb, tq), 1)) // SEL_BLOCK
    forced = (blk1 == 0) | (blk1 == cur) | (blk1 == cur - 1)
    visible = blk1 <= cur

    for h in range(NSA_KV):
        tiles = []
        for g in range(NSA_G):
            c = h * NSA_G + g
            t = qf[:, (c // 2) * LANES:(c // 2 + 1) * LANES]
            if c % 2 != h:
                t = pltpu.roll(t, HEAD_DIM, axis=1)
            tiles.append(jnp.where(lane_head == h, t, 0.0))
        qs = jnp.concatenate(tiles, axis=0).astype(BF)

        s = jnp.where(vis_c, _dot_nt(kcb, qs), NEG)
        m = jnp.max(s, axis=0, keepdims=True)
        e = jnp.where(vis_c, jnp.exp(s - m), 0.0)
        l = jnp.sum(e, axis=0, keepdims=True)
        p_c = e / jnp.where(l > 0.0, l, 1.0)
        o_c = _dot(vct[h * HEAD_DIM:(h + 1) * HEAD_DIM], p_c.astype(BF))

        imp = p_c[:, 0:tq]
        for g in range(1, NSA_G):
            imp = imp + p_c[:, g * tq:(g + 1) * tq]
        sel = _select_blocks(imp, forced, visible, TOP_K - 3)
        bias_ref[...] = jnp.where(sel, 0.0, NEG)

        def chunk(c, carry, causal):
            m_i, l_i, acc = carry
            k0 = pl.multiple_of(c * tk, tk)
            s = _dot_nt(ks_ref[pl.ds(k0, tk), :], qs)
            rows = []
            for r in range(tk // SEL_BLOCK):
                b = bias_ref[pl.ds(c * (tk // SEL_BLOCK) + r, 1), :]
                b = jnp.concatenate([b] * NSA_G, axis=1)
                rows.append(jnp.broadcast_to(b, (SEL_BLOCK, wl)))
            s = s + jnp.concatenate(rows, axis=0)
            if causal:
                kpos = k0 + lax.broadcasted_iota(jnp.int32, (tk, wl), 0)
                s = jnp.where(kpos <= qpos, s, NEG)
            m_n = jnp.maximum(m_i, jnp.max(s, axis=0, keepdims=True))
            alpha = jnp.exp(m_i - m_n)
            p = jnp.exp(s - m_n)
            l_n = alpha * l_i + jnp.sum(p, axis=0, keepdims=True)
            vt = vst_ref[0, c, h * HEAD_DIM:(h + 1) * HEAD_DIM, :]
            acc_n = alpha * acc + _dot(vt, p.astype(BF))
            return m_n, l_n, acc_n

        init = (jnp.full((1, wl), NEG, F32), jnp.zeros((1, wl), F32), jnp.zeros((HEAD_DIM, wl), F32))
        n_full = q0 // tk
        carry = lax.fori_loop(0, n_full, functools.partial(chunk, causal=False), init)
        m_s, l_s, acc_s = chunk(n_full, carry, True)
        o_s = acc_s / l_s

        n_piece = WINDOW // LANES + 1
        s_parts = []
        for j in range(n_piece):
            start = q0 - WINDOW + j * LANES
            cl = jnp.maximum(start, 0)
            sw = _dot_nt(kw_ref[pl.ds(pl.multiple_of(cl, LANES), LANES), :], qs)
            kpos = start + lax.broadcasted_iota(jnp.int32, (LANES, wl), 0)
            d = qpos - kpos
            ok = (d >= 0) & (d < WINDOW) & (kpos >= 0)
            s_parts.append(jnp.where(ok, sw, NEG))
        sw = jnp.concatenate(s_parts, axis=0)
        mw = jnp.max(sw, axis=0, keepdims=True)
        pw = jnp.exp(sw - mw)
        lw = jnp.sum(pw, axis=0, keepdims=True)
        pwb = pw.astype(BF)
        o_w = jnp.zeros((HEAD_DIM, wl), F32)
        for j in range(n_piece):
            cj = jnp.maximum(i - WINDOW // LANES + j, 0)
            vt = vwt_ref[0, cj, h * HEAD_DIM:(h + 1) * HEAD_DIM, :]
            o_w = o_w + _dot(vt, pwb[j * LANES:(j + 1) * LANES])
        o_w = o_w / lw

        def gate_row(j):
            return jnp.concatenate(
                [gate_t[(h * NSA_G + g) * 3 + j:(h * NSA_G + g) * 3 + j + 1, :] for g in range(NSA_G)], axis=1)

        o_t = gate_row(0) * o_c + gate_row(1) * o_s + gate_row(2) * o_w
        for gp in range(NSA_G // 2):
            pair = jnp.concatenate([o_t[:, (2 * gp) * tq:(2 * gp + 1) * tq],
                                    o_t[:, (2 * gp + 1) * tq:(2 * gp + 2) * tq]], axis=0)
            c0 = (h * (NSA_G // 2) + gp) * LANES
            out_ref[:, c0:c0 + LANES] = pair.T * _silu(nz_ref[:, c0:c0 + LANES])


def _nsa_prompt(q, misc, nz, cmp, ksb, vst, kwb, vwt, *, batch, seq):
    tq = NSA_TQ
    nq = seq // tq
    nb = seq // SEL_BLOCK
    row = lambda b, i: (b * nq + i, 0)
    return pl.pallas_call(
        functools.partial(_nsa_prompt_body, seq=seq),
        grid=(batch, nq),
        in_specs=[pl.BlockSpec((tq, NSA_WIDTH), row), pl.BlockSpec((tq, LANES), row),
                  pl.BlockSpec((tq, NSA_WIDTH), row),
                  pl.BlockSpec((1, nb, KVW), lambda b, i: (b, 0, 0)),
                  pl.BlockSpec((seq, HALF), lambda b, i: (b, 0)),
                  pl.BlockSpec((1, seq // PROJ_TM, HALF, PROJ_TM), lambda b, i: (b, 0, 0, 0)),
                  pl.BlockSpec((seq, HALF), lambda b, i: (b, 0)),
                  pl.BlockSpec((1, seq // LANES, HALF, LANES), lambda b, i: (b, 0, 0, 0))],
        out_specs=pl.BlockSpec((tq, NSA_WIDTH), row),
        out_shape=jax.ShapeDtypeStruct((batch * seq, NSA_WIDTH), F32),
        scratch_shapes=[pltpu.VMEM((nb, tq), F32)],
        compiler_params=_cparams(2),
        name="nsa_prompt",
    )(q, misc, nz, cmp.reshape(batch, nb, KVW), ksb, vst, kwb, vwt)


def _cmp_past_body(pt_ref, *refs):
    del pt_ref
    pages, wc_ref, out_ref = refs[:-2], refs[-2], refs[-1]
    rows = jnp.concatenate([pg[0, 0] for pg in pages], axis=0)
    blocks = rows.reshape(rows.shape[0] // CMP_BLOCK, CMP_BLOCK, KVW)
    out_ref[0] = jnp.sum(blocks * wc_ref[...][None], axis=1)


def _page_spec(layer, page_size, r, pps):
    return pl.BlockSpec((1, 1, page_size, KVW), lambda b, c, pt: (layer, pt[b, c * pps + r], 0, 0))


def _cmp_past(cache, page_table, wc, *, layer):
    n_layers, n_pool, page_size = cache.shape[:3]
    bd, n_pages = page_table.shape
    pps = PAGES_PER_STEP
    per_page = page_size // CMP_BLOCK
    cache4 = cache.reshape(n_layers, n_pool, page_size, KVW)
    grid_spec = pltpu.PrefetchScalarGridSpec(
        num_scalar_prefetch=1,
        grid=(bd, n_pages // pps),
        in_specs=[_page_spec(layer, page_size, r, pps) for r in range(pps)]
        + [pl.BlockSpec((CMP_BLOCK, KVW), lambda b, c, pt: (0, 0))],
        out_specs=pl.BlockSpec((1, pps * per_page, KVW), lambda b, c, pt: (b, c, 0)),
    )
    return pl.pallas_call(
        _cmp_past_body,
        grid_spec=grid_spec,
        out_shape=jax.ShapeDtypeStruct((bd, n_pages * per_page, KVW), F32),
        compiler_params=_cparams(2),
        name="cmp_past",
    )(page_table, *([cache4] * pps), wc)


def _nsa_sample_body(pt_ref, *refs, t_new, past_len, page_size):
    del pt_ref
    pps = PAGES_PER_STEP
    pages = refs[:pps]
    (q_ref, misc_ref, nz_ref, cmp_ref, kvs_ref, kvw_ref, win_ref, out_ref, newwin_ref,
     qs_ref, masked_ref, m_ref, l_ref, acc_ref) = refs[pps:]
    c = pl.program_id(1)
    n_chunks = pl.num_programs(1)
    used = NSA_HEADS * t_new
    rows = LANES
    nbp = past_len // SEL_BLOCK
    keys = pps * page_size
    t_of_row = lax.broadcasted_iota(jnp.int32, (rows, 1), 0) % t_new

    @pl.when(c == 0)
    def _():
        qf = q_ref[...] * (HEAD_DIM ** -0.5)
        lane_head = lax.broadcasted_iota(jnp.int32, (t_new, LANES), 1) // HEAD_DIM
        tiles = []
        for hg in range(NSA_HEADS):
            h = hg // NSA_G
            t = qf[:, (hg // 2) * LANES:(hg // 2 + 1) * LANES]
            if hg % 2 != h:
                t = pltpu.roll(t, HEAD_DIM, axis=1)
            tiles.append(jnp.where(lane_head == h, t, 0.0))
        tiles.append(jnp.zeros((rows - used, LANES), F32))
        qs = jnp.concatenate(tiles, axis=0)
        qs_ref[...] = qs
        kc = cmp_ref[0]
        s = _dot_nt(kc[:, :HALF].astype(BF), qs.astype(BF))
        m = jnp.max(s, axis=0, keepdims=True)
        e = jnp.exp(s - m)
        p_c = e / jnp.sum(e, axis=0, keepdims=True)
        o_c = _dot(kc[:, HALF:].T.astype(BF), p_c.astype(BF))
        acc_ref[1] = o_c.T
        ri = lax.broadcasted_iota(jnp.int32, (rows, rows), 0)
        ci = lax.broadcasted_iota(jnp.int32, (rows, rows), 1)
        same = ((ri // (NSA_G * t_new)) == (ci // (NSA_G * t_new))) & ((ri % t_new) == (ci % t_new))
        gsum = jnp.where(same, 1.0, 0.0).astype(BF)
        p0, p1, p2 = _split3(p_c)
        imp = _dot(p0, gsum) + _dot(p1, gsum) + _dot(p2, gsum)
        blk = lax.broadcasted_iota(jnp.int32, (nbp, rows), 0)
        forced = (blk == 0) | (blk == nbp - 1)
        sel = _select_blocks(imp, forced, blk >= 0, TOP_K - 3)
        masked_ref[...] = jnp.where(sel, 0.0, 1.0).T
        m_ref[...] = jnp.full(m_ref.shape, NEG, F32)
        l_ref[...] = jnp.zeros(l_ref.shape, F32)
        acc_ref[0] = jnp.zeros(acc_ref.shape[1:], F32)

    qsb = qs_ref[...].astype(BF)
    eb = lax.broadcasted_iota(jnp.int32, (nbp, keys), 0)
    ek = lax.broadcasted_iota(jnp.int32, (nbp, keys), 1) // SEL_BLOCK + c * (keys // SEL_BLOCK)
    expand = jnp.where(eb == ek, 1.0, 0.0).astype(BF)
    bias = _dot(masked_ref[...].astype(BF), expand) * NEG
    s = jnp.concatenate([_dot_nt(qsb, pg[0, 0, :, :HALF].astype(BF)) for pg in pages], axis=1) + bias
    m_i = m_ref[...]
    m_n = jnp.maximum(m_i, jnp.max(s, axis=1, keepdims=True))
    alpha = jnp.exp(m_i - m_n)
    p = jnp.exp(s - m_n)
    l_ref[...] = alpha * l_ref[...] + jnp.sum(p, axis=1, keepdims=True)
    pb = p.astype(BF)
    acc = alpha * acc_ref[0]
    for r, pg in enumerate(pages):
        acc = acc + _dot(pb[:, r * page_size:(r + 1) * page_size], pg[0, 0, :, HALF:].astype(BF))
    acc_ref[0] = acc
    m_ref[...] = m_n

    @pl.when(c == n_chunks - 1)
    def _():
        kvs = kvs_ref[...]
        tn = lax.broadcasted_iota(jnp.int32, (rows, t_new), 1)
        s_n = jnp.where(tn <= t_of_row, _dot_nt(qsb, kvs[:, :HALF].astype(BF)), NEG)
        m_i = m_ref[...]
        m_n = jnp.maximum(m_i, jnp.max(s_n, axis=1, keepdims=True))
        alpha = jnp.exp(m_i - m_n)
        p_n = jnp.exp(s_n - m_n)
        l_s = alpha * l_ref[...] + jnp.sum(p_n, axis=1, keepdims=True)
        o_s = (alpha * acc_ref[0] + _dot(p_n.astype(BF), kvs[:, HALF:].astype(BF))) / l_s
        wb = win_ref[0, 0]
        w_buf = wb.shape[0]
        kvw = kvw_ref[...]
        rb = lax.broadcasted_iota(jnp.int32, (rows, w_buf), 1)
        s_b = jnp.where(w_buf + t_of_row - rb < WINDOW, _dot_nt(qsb, wb[:, :HALF].astype(BF)), NEG)
        s_w = jnp.where(tn <= t_of_row, _dot_nt(qsb, kvw[:, :HALF].astype(BF)), NEG)
        m_w = jnp.maximum(jnp.max(s_b, axis=1, keepdims=True), jnp.max(s_w, axis=1, keepdims=True))
        p_b = jnp.exp(s_b - m_w)
        p_w = jnp.exp(s_w - m_w)
        l_w = jnp.sum(p_b, axis=1, keepdims=True) + jnp.sum(p_w, axis=1, keepdims=True)
        o_w = (_dot(p_b.astype(BF), wb[:, HALF:].astype(BF)) + _dot(p_w.astype(BF), kvw[:, HALF:].astype(BF))) / l_w
        newwin_ref[0, :w_buf - t_new, :] = wb[t_new:, :]
        newwin_ref[0, w_buf - t_new:, :] = kvw
        gates = _sigmoid(misc_ref[...])
        o_c = acc_ref[1]
        nz = nz_ref[...]
        for hg in range(NSA_HEADS):
            h = hg // NSA_G
            r0 = hg * t_new
            o = (gates[:, hg * 3:hg * 3 + 1] * o_c[r0:r0 + t_new]
                 + gates[:, hg * 3 + 1:hg * 3 + 2] * o_s[r0:r0 + t_new]
                 + gates[:, hg * 3 + 2:hg * 3 + 3] * o_w[r0:r0 + t_new])
            c0 = hg * HEAD_DIM
            out_ref[:, c0:c0 + HEAD_DIM] = o[:, h * HEAD_DIM:(h + 1) * HEAD_DIM] * _silu(nz[:, c0:c0 + HEAD_DIM])


def _nsa_sample(q, misc, nz, cmp_past, kvs, kvw, cache_slc, win_buf, page_table, *, layer, t_new):
    n_layers, n_pool, page_size = cache_slc.shape[:3]
    bd, n_pages = page_table.shape
    past_len = n_pages * page_size
    w_buf = win_buf.shape[2]
    pps = PAGES_PER_STEP
    nbp = past_len // SEL_BLOCK
    cache4 = cache_slc.reshape(n_layers, n_pool, page_size, KVW)
    win4 = win_buf.reshape(n_layers, bd, w_buf, KVW)
    row = lambda b, c, pt: (b, 0)
    grid_spec = pltpu.PrefetchScalarGridSpec(
        num_scalar_prefetch=1,
        grid=(bd, n_pages // pps),
        in_specs=[_page_spec(layer, page_size, r, pps) for r in range(pps)]
        + [pl.BlockSpec((t_new, NSA_WIDTH), row), pl.BlockSpec((t_new, LANES), row),
           pl.BlockSpec((t_new, NSA_WIDTH), row),
           pl.BlockSpec((1, nbp, KVW), lambda b, c, pt: (b, 0, 0)),
           pl.BlockSpec((t_new, KVW), row), pl.BlockSpec((t_new, KVW), row),
           pl.BlockSpec((1, 1, w_buf, KVW), lambda b, c, pt: (layer, b, 0, 0))],
        out_specs=[pl.BlockSpec((t_new, NSA_WIDTH), row),
                   pl.BlockSpec((1, w_buf, KVW), lambda b, c, pt: (b, 0, 0))],
        scratch_shapes=[pltpu.VMEM((LANES, LANES), F32), pltpu.VMEM((LANES, nbp), F32),
                        pltpu.VMEM((LANES, 1), F32), pltpu.VMEM((LANES, 1), F32),
                        pltpu.VMEM((2, LANES, LANES), F32)],
    )
    return pl.pallas_call(
        functools.partial(_nsa_sample_body, t_new=t_new, past_len=past_len, page_size=page_size),
        grid_spec=grid_spec,
        out_shape=[jax.ShapeDtypeStruct((bd * t_new, NSA_WIDTH), F32),
                   jax.ShapeDtypeStruct((bd, w_buf, KVW), F32)],
        compiler_params=_cparams(2),
        name="nsa_sample",
    )(page_table, *([cache4] * pps), q, misc, nz, cmp_past, kvs, kvw, win4)


def _gla_body(gla_ref, misc_ref, a2_ref, ab_ref, gn_ref, s0_ref, out_ref, sout_ref, st_ref, *, chunk, mxu_dtype):
    j = pl.program_id(1)
    n_steps = pl.num_programs(1)
    tb = gla_ref.shape[0]
    sub = min(GLA_SUB, chunk)
    hk = GLA_HEADS * chunk

    @pl.when(j == 0)
    def _():
        s0 = s0_ref[0].reshape(GLA_KW, GLA_DV)
        ri = lax.broadcasted_iota(jnp.int32, (GLA_WIDTH, GLA_DV), 0) % GLA_DV
        ci = lax.broadcasted_iota(jnp.int32, (GLA_WIDTH, GLA_DV), 1)
        pick = jnp.where(ri == ci, 1.0, 0.0).astype(BF)
        wide = sum(_dot_nt(pick, part) for part in _split3(s0))
        st_ref[...] = jnp.where(_gla_diag(GLA_WIDTH, GLA_DV, GLA_KW, GLA_DK), wide, 0.0)

    diag_sv = _gla_diag(GLA_WIDTH, GLA_DV, GLA_KW, GLA_DK)
    diag_k = _gla_diag(hk, chunk, GLA_KW, GLA_DK)
    diag_v = _gla_diag(hk, chunk, GLA_WIDTH, GLA_DV)
    tril = jnp.where(lax.broadcasted_iota(jnp.int32, (chunk, chunk), 0)
                     >= lax.broadcasted_iota(jnp.int32, (chunk, chunk), 1), 1.0, 0.0).astype(BF)
    srow = lax.broadcasted_iota(jnp.int32, (chunk, GLA_KW), 0)
    ri = lax.broadcasted_iota(jnp.int32, (GLA_WIDTH, GLA_WIDTH), 0) // GLA_DV
    ci = lax.broadcasted_iota(jnp.int32, (GLA_WIDTH, GLA_WIDTH), 1) // GLA_DV
    head_mean = jnp.where(ri == ci, 1.0 / GLA_DV, 0.0).astype(BF)
    a_col = lax.broadcasted_iota(jnp.int32, (sub, hk), 1) % chunk
    a_row = lax.broadcasted_iota(jnp.int32, (sub, hk), 0)

    def cast(a):
        return a.astype(mxu_dtype)

    def one_chunk(ci_, carry):
        r0 = pl.multiple_of(ci_ * chunk, chunk)
        blk = gla_ref[pl.ds(r0, chunk), :]
        gq = blk[:, 0:GLA_KW] * (GLA_DK ** -0.5)
        gk = blk[:, GLA_KW:2 * GLA_KW]
        gv = blk[:, 2 * GLA_KW:2 * GLA_KW + GLA_WIDTH]
        gz = blk[:, 2 * GLA_KW + GLA_WIDTH:]
        a_logit = _dot(cast(misc_ref[pl.ds(r0, chunk), :]), cast(a2_ref[...])) + ab_ref[...]
        log_a = (jnp.minimum(a_logit, 0.0) - jnp.log(1.0 + jnp.exp(-jnp.abs(a_logit)))) * (1.0 / GLA_TAU)
        la0, la1, la2 = _split3(log_a)
        bcum = _dot(tril, la0) + _dot(tril, la1) + _dot(tril, la2)
        st = st_ref[...]
        inter = _dot_nt(cast(gq * jnp.exp(bcum)), cast(st))
        v_bd = jnp.where(diag_v, jnp.concatenate([gv] * GLA_HEADS, axis=0), 0.0)
        v_bd = cast(v_bd)
        parts = []
        for sb in range(chunk // sub):
            t0 = sb * sub
            ref_row = bcum[t0:t0 + 1, :]
            qd = gq[t0:t0 + sub] * jnp.exp(bcum[t0:t0 + sub] - ref_row)
            kd = gk * jnp.exp(jnp.where(srow < t0 + sub, ref_row - bcum, NEG))
            k_bd = jnp.where(diag_k, jnp.concatenate([kd] * GLA_HEADS, axis=0), 0.0)
            att = _dot_nt(cast(qd), cast(k_bd))
            att = jnp.where(a_col <= a_row + t0, att, 0.0)
            parts.append(_dot(cast(att), v_bd))
        o = inter + jnp.concatenate(parts, axis=0)
        b_last = bcum[chunk - 1:chunk, :]
        kdec = gk * jnp.exp(b_last - bcum)
        upd = _dot(cast(_transpose_mxu(gv, mxu_dtype)), cast(kdec))
        st_ref[...] = st * jnp.exp(b_last) + jnp.where(diag_sv, upd, 0.0)
        o2 = o * o
        q0, q1, _ = _split3(o2)
        ms = _dot(q0, head_mean) + _dot(q1, head_mean)
        out_ref[pl.ds(r0, chunk), :] = o * lax.rsqrt(ms + EPS) * gn_ref[...] * _silu(gz)
        return carry

    lax.fori_loop(0, tb // chunk, one_chunk, 0)

    @pl.when(j == n_steps - 1)
    def _():
        st = st_ref[...]
        acc = st[0:GLA_DV]
        for h in range(1, GLA_HEADS):
            acc = acc + st[h * GLA_DV:(h + 1) * GLA_DV]
        sout_ref[0] = _transpose_mxu(acc, F32).reshape(GLA_HEADS, GLA_DK, GLA_DV)


def _gla_diag(n_rows, row_group, n_cols, col_group):
    r = lax.broadcasted_iota(jnp.int32, (n_rows, n_cols), 0) // row_group
    c = lax.broadcasted_iota(jnp.int32, (n_rows, n_cols), 1) // col_group
    return r == c


def _gla(gla, misc, a2p, ab, gn, s0, *, batch, seq):
    chunk = min(GLA_CHUNK, seq)
    tb = min(PROJ_TM, seq)
    steps = seq // tb
    row = lambda b, j: (b * steps + j, 0)
    fixed = lambda b, j: (0, 0)
    return pl.pallas_call(
        functools.partial(_gla_body, chunk=chunk, mxu_dtype=BF if chunk >= 16 else F32),
        grid=(batch, steps),
        in_specs=[pl.BlockSpec((tb, W_GLA), row), pl.BlockSpec((tb, LANES), row),
                  pl.BlockSpec((LANES, GLA_KW), fixed), pl.BlockSpec((1, GLA_KW), fixed),
                  pl.BlockSpec((1, GLA_WIDTH), fixed),
                  pl.BlockSpec((1, GLA_HEADS, GLA_DK, GLA_DV), lambda b, j: (b, 0, 0, 0))],
        out_specs=[pl.BlockSpec((tb, GLA_WIDTH), row),
                   pl.BlockSpec((1, GLA_HEADS, GLA_DK, GLA_DV), lambda b, j: (b, 0, 0, 0))],
        out_shape=[jax.ShapeDtypeStruct((batch * seq, GLA_WIDTH), F32),
                   jax.ShapeDtypeStruct((batch, GLA_HEADS, GLA_DK, GLA_DV), F32)],
        scratch_shapes=[pltpu.VMEM((GLA_WIDTH, GLA_KW), F32)],
        compiler_params=_cparams(2),
        name="gla_prompt" if seq > GLA_CHUNK else "gla_sample",
    )(gla, misc, a2p, ab, gn, s0)


def _out_body(x_ref, nsa_ref, gla_ref, conv_ref, halo_ref, buf_ref, cw_ref, wo_ref, g_ref, y_ref, nbuf_ref,
              *, seq, tm):
    cv = conv_ref[...]
    cb = cv[:, 0:CONV_WIDTH]
    u = cv[:, CONV_WIDTH:2 * CONV_WIDTH] * cv[:, 2 * CONV_WIDTH:3 * CONV_WIDTH]
    cz = cv[:, 3 * CONV_WIDTH:]
    rows = lax.broadcasted_iota(jnp.int32, (tm, CONV_WIDTH), 0)
    if seq >= tm:
        first = pl.program_id(0) % (seq // tm) == 0
        hv = halo_ref[...]
        hu = hv[:, CONV_WIDTH:2 * CONV_WIDTH] * hv[:, 2 * CONV_WIDTH:3 * CONV_WIDTH]
        bufv = buf_ref[0]
        p1 = jnp.where(first, bufv[1:2], hu[7:8])
        p2 = jnp.where(first, bufv[0:1], hu[6:7])
        prev1 = jnp.where(rows == 0, p1, pltpu.roll(u, 1, axis=0))
        prev2 = jnp.where(rows == 0, p2, jnp.where(rows == 1, p1, pltpu.roll(u, 2, axis=0)))
        nbuf_ref[0] = u[tm - (CONV_K - 1):, :]
    else:
        nseq = tm // seq
        bufv = buf_ref[...]
        b0 = jnp.broadcast_to(bufv[:, 0:1, :], (nseq, seq, CONV_WIDTH)).reshape(tm, CONV_WIDTH)
        b1 = jnp.broadcast_to(bufv[:, 1:2, :], (nseq, seq, CONV_WIDTH)).reshape(tm, CONV_WIDTH)
        t = rows % seq
        prev1 = jnp.where(t == 0, b1, pltpu.roll(u, 1, axis=0))
        prev2 = jnp.where(t == 0, b0, jnp.where(t == 1, b1, pltpu.roll(u, 2, axis=0)))
        nbuf_ref[...] = u.reshape(nseq, seq, CONV_WIDTH)[:, seq - (CONV_K - 1):, :]
    cw = cw_ref[...]
    y = prev2 * cw[0:1] + prev1 * cw[1:2] + u * cw[2:3]
    o_conv = cb * y * _silu(cz)
    acc = _dot(nsa_ref[...].astype(BF), wo_ref[0:NSA_WIDTH, :])
    acc = acc + _dot(gla_ref[...].astype(BF), wo_ref[NSA_WIDTH:NSA_WIDTH + GLA_WIDTH, :])
    acc = acc + _dot(o_conv.astype(BF), wo_ref[NSA_WIDTH + GLA_WIDTH:, :])
    normed = acc * lax.rsqrt(jnp.mean(acc * acc, axis=-1, keepdims=True) + EPS) * g_ref[...]
    y_ref[...] = x_ref[...] + normed


def _out(x2, o_nsa, o_gla, conv, buf, cw, wo, g, *, batch, seq):
    n, d = x2.shape
    tm = PROJ_TM if seq >= PROJ_TM else n
    steps = n // tm
    row = lambda i: (i, 0)
    fixed = lambda i: (0, 0)
    if seq >= tm:
        per_seq = seq // tm
        halo_spec = pl.BlockSpec((8, W_CONV), lambda i: (jnp.maximum(i * (tm // 8) - 1, 0), 0))
        buf_spec = pl.BlockSpec((1, CONV_K - 1, CONV_WIDTH), lambda i: (i // per_seq, 0, 0))
    else:
        halo_spec = pl.BlockSpec((8, W_CONV), fixed)
        buf_spec = pl.BlockSpec((batch, CONV_K - 1, CONV_WIDTH), lambda i: (0, 0, 0))
    return pl.pallas_call(
        functools.partial(_out_body, seq=seq, tm=tm),
        grid=(steps,),
        in_specs=[pl.BlockSpec((tm, d), row), pl.BlockSpec((tm, NSA_WIDTH), row),
                  pl.BlockSpec((tm, GLA_WIDTH), row), pl.BlockSpec((tm, W_CONV), row),
                  halo_spec, buf_spec,
                  pl.BlockSpec((CONV_K, CONV_WIDTH), fixed), pl.BlockSpec((d, d), fixed),
                  pl.BlockSpec((1, d), fixed)],
        out_specs=[pl.BlockSpec((tm, d), row), buf_spec],
        out_shape=[jax.ShapeDtypeStruct((n, d), F32),
                   jax.ShapeDtypeStruct((batch, CONV_K - 1, CONV_WIDTH), F32)],
        compiler_params=_cparams(1),
        name="out_prompt" if seq >= PROJ_TM else "out_sample",
    )(x2, o_nsa, o_gla, conv, conv, buf, cw, wo, g)


def _reorder_w_in(w):
    o_gate = NSA_WIDTH + 3 * KVW
    o_nz = o_gate + N_GATE
    o_gla = o_nz + NSA_WIDTH
    o_a = o_gla + 2 * GLA_KW + GLA_WIDTH
    o_gz = o_a + GLA_RANK
    o_conv = o_gz + GLA_WIDTH
    pad = jnp.zeros((w.shape[0], LANES - N_GATE - GLA_RANK), w.dtype)
    return jnp.concatenate([w[:, :o_gate], w[:, o_nz:o_gla], w[:, o_gla:o_a], w[:, o_gz:o_conv], w[:, o_conv:],
                            w[:, o_gate:o_nz], w[:, o_a:o_gz], pad], axis=1)


def _layer(x, l, P, *, prompt, cache_cmp_kv=None, cache_slc_kv=None, cache_win_kv=None, page_table=None,
           gla_s0=None, conv_buf=None):
    batch, seq, d = x.shape
    x2 = x.reshape(batch * seq, d)
    outs = _proj(x2, P['norm_pre'][l], P['w_in'][l], P['w_cmp'][l], batch=batch, seq=seq, prompt=prompt)
    q, kvc, kvs, kvw, nz, gla, conv, misc = outs[:8]
    kv_shape = (batch, seq, 2, NSA_KV, HEAD_DIM)
    if prompt:
        ksb, kwb, vst, vwt, cmp = outs[8:]
        o_nsa = _nsa_prompt(q, misc, nz, cmp, ksb, vst, kwb, vwt, batch=batch, seq=seq)
        w_keep = min(WINDOW, seq)
        new_w = kvw.reshape(kv_shape)[:, seq - w_keep:]
    else:
        cmp_past = _cmp_past(cache_cmp_kv, page_table, P['w_cmp'][l], layer=l)
        o_nsa, new_w = _nsa_sample(q, misc, nz, cmp_past, kvs, kvw, cache_slc_kv, cache_win_kv, page_table,
                                   layer=l, t_new=seq)
        new_w = new_w.reshape((batch, -1) + kv_shape[2:])
    o_gla, s_gla = _gla(gla, misc, P['gla_a2'][l], P['gla_ab'][l], P['gla_norm'][l], gla_s0, batch=batch, seq=seq)
    y2, new_buf = _out(x2, o_nsa, o_gla, conv, conv_buf, P['conv_w'][l], P['w_out'][l], P['norm_post'][l],
                       batch=batch, seq=seq)
    return y2.reshape(batch, seq, d), (kvc.reshape(kv_shape), kvs.reshape(kv_shape), new_w, s_gla, new_buf)


def kernel(x_prompt, x_sample, cache_cmp_kv, cache_slc_kv, cache_win_kv, state_gla, state_conv, page_table,
           norm_pre, norm_post, w_in, w_out, w_cmp_k, w_cmp_v, gla_a2, gla_ab, gla_norm, conv_w):
    depth = w_in.shape[0]
    d = w_in.shape[1]
    ones = jnp.ones((1, HALF), F32)
    a2p = jnp.zeros((depth, LANES, GLA_KW), F32).at[:, N_GATE:N_GATE + GLA_RANK, :].set(gla_a2)
    P = {
        'norm_pre': norm_pre.reshape(depth, 1, d),
        'norm_post': norm_post.reshape(depth, 1, d),
        'w_in': jax.vmap(_reorder_w_in)(w_in).astype(BF),
        'w_out': w_out.astype(BF),
        'w_cmp': jnp.concatenate([w_cmp_k[:, :, None] * ones, w_cmp_v[:, :, None] * ones], axis=2),
        'gla_a2': a2p,
        'gla_ab': gla_ab.reshape(depth, 1, GLA_KW),
        'gla_norm': jnp.tile(gla_norm, (1, GLA_HEADS)).reshape(depth, 1, GLA_WIDTH),
        'conv_w': conv_w,
    }
    xp, xs = x_prompt, x_sample
    bp = xp.shape[0]
    st_p, st_s = [], []
    for l in range(depth):
        xp, sp = _layer(xp, l, P, prompt=True,
                        gla_s0=jnp.zeros((bp, GLA_HEADS, GLA_DK, GLA_DV), F32),
                        conv_buf=jnp.zeros((bp, CONV_K - 1, CONV_WIDTH), F32))
        xs, ss = _layer(xs, l, P, prompt=False, cache_cmp_kv=cache_cmp_kv, cache_slc_kv=cache_slc_kv,
                        cache_win_kv=cache_win_kv, page_table=page_table,
                        gla_s0=state_gla[l], conv_buf=state_conv[l])
        st_p.append(sp)
        st_s.append(ss)
    stack = lambda sts, k: jnp.stack([s[k] for s in sts])
    return (xp, xs, stack(st_p, 0), stack(st_s, 0), stack(st_p, 1), stack(st_s, 1), stack(st_p, 2), stack(st_s, 2),
            stack(st_p, 3), stack(st_s, 3), stack(st_p, 4), stack(st_s, 4))
```

```python
import functools

import jax
import jax.numpy as jnp
from jax import lax
from jax.experimental import pallas as pl
from jax.experimental.pallas import tpu as pltpu

F32 = jnp.float32
BF = jnp.bfloat16

HEAD_DIM = 64
NSA_KV = 2
NSA_G = 4
NSA_HEADS = NSA_KV * NSA_G
NSA_WIDTH = NSA_HEADS * HEAD_DIM
KVW = 2 * NSA_KV * HEAD_DIM
HALF = NSA_KV * HEAD_DIM
CMP_BLOCK = 64
SEL_BLOCK = 64
TOP_K = 16
WINDOW = 512
GLA_HEADS = 4
GLA_DK = 32
GLA_DV = 64
GLA_KW = GLA_HEADS * GLA_DK
GLA_WIDTH = GLA_HEADS * GLA_DV
GLA_RANK = 16
GLA_TAU = 16.0
GLA_CHUNK = 64
GLA_SUB = 16
CONV_WIDTH = 256
CONV_K = 3
N_GATE = 3 * NSA_HEADS
EPS = 1e-6
NEG = -1e30
LOG2E = 1.4426950408889634

LANES = 128
VMEM_LIMIT = 48 * 1024 * 1024

C_Q = 0
C_CMP = C_Q + NSA_WIDTH
C_SLC = C_CMP + KVW
C_WIN = C_SLC + KVW
C_NZ = C_WIN + KVW
C_GLA = C_NZ + NSA_WIDTH
W_GLA = 2 * GLA_KW + 2 * GLA_WIDTH
C_CONV = C_GLA + W_GLA
W_CONV = 4 * CONV_WIDTH
C_MISC = C_CONV + W_CONV
PROJ_PAD = C_MISC + LANES

PROJ_TM = 512
NSA_TQ = 128
PAGES_PER_STEP = 16


def _cparams(n_axes, vmem=VMEM_LIMIT):
    return pltpu.CompilerParams(dimension_semantics=("arbitrary",) * n_axes, vmem_limit_bytes=vmem)


def _dot(a, b):
    return jnp.dot(a, b, preferred_element_type=F32)


def _dot_nt(a, b):
    return lax.dot_general(a, b, (((1,), (1,)), ((), ())), preferred_element_type=F32)


def _split3(a):
    a0 = a.astype(BF)
    r = a - a0.astype(F32)
    a1 = r.astype(BF)
    a2 = (r - a1.astype(F32)).astype(BF)
    return a0, a1, a2


def _transpose_mxu(y, dtype):
    n = y.shape[1]
    eye = jnp.where(lax.broadcasted_iota(jnp.int32, (n, n), 0) == lax.broadcasted_iota(jnp.int32, (n, n), 1),
                    1.0, 0.0).astype(BF)
    if dtype == BF:
        return _dot_nt(eye, y.astype(BF))
    return sum(_dot_nt(eye, part) for part in _split3(y))


def _silu(x):
    return x * (1.0 / (1.0 + jnp.exp(-x)))


def _sigmoid(x):
    return 1.0 / (1.0 + jnp.exp(-x))


def _proj_body(x_ref, g_ref, w_ref, wc_ref, *outs, prompt, seq):
    x = x_ref[...]
    h = x * lax.rsqrt(jnp.mean(x * x, axis=-1, keepdims=True) + EPS) * g_ref[...]
    hb = h.astype(BF)

    def proj(c0, n):
        return _dot(hb, w_ref[:, c0:c0 + n])

    q_ref, kvc_ref, kvs_ref, kvw_ref, nz_ref, gla_ref, conv_ref, misc_ref = outs[:8]
    q_ref[...] = proj(C_Q, NSA_WIDTH)
    kvc = proj(C_CMP, KVW)
    kvs = proj(C_SLC, KVW)
    kvw = proj(C_WIN, KVW)
    kvc_ref[...] = kvc
    kvs_ref[...] = kvs
    kvw_ref[...] = kvw
    nz_ref[...] = proj(C_NZ, NSA_WIDTH)
    gla_ref[...] = proj(C_GLA, W_GLA)
    conv_ref[...] = proj(C_CONV, W_CONV)
    misc_ref[...] = proj(C_MISC, LANES)
    if prompt:
        ksb_ref, kwb_ref, vst_ref, vwt_ref, cmp_ref = outs[8:]
        tm = x.shape[0]
        blk = (pl.program_id(0) % (seq // tm)) * (tm // SEL_BLOCK) \
            + lax.broadcasted_iota(jnp.int32, (tm, LANES), 0) // SEL_BLOCK
        onehot = jnp.where(lax.broadcasted_iota(jnp.int32, (tm, LANES), 1) == blk, 1.0, 0.0)
        ksb_ref[...] = jnp.concatenate([kvs[:, :HALF], onehot], axis=1).astype(BF)
        kwb_ref[...] = kvw[:, :HALF].astype(BF)
        vst_ref[0, 0] = kvs[:, HALF:].T.astype(BF)
        for j in range(tm // LANES):
            vwt_ref[0, j] = kvw[j * LANES:(j + 1) * LANES, HALF:].T.astype(BF)
        cmp_ref[...] = jnp.sum(kvc.reshape(tm // CMP_BLOCK, CMP_BLOCK, KVW) * wc_ref[...][None], axis=1)


def _proj(x2, g, w, wc, *, batch, seq, prompt):
    n, d = x2.shape
    tm = PROJ_TM if prompt else n
    steps = n // tm
    per_seq = seq // tm if prompt else 1
    row = lambda i: (i, 0)
    fixed = lambda i: (0, 0)
    widths = [NSA_WIDTH, KVW, KVW, KVW, NSA_WIDTH, W_GLA, W_CONV, LANES]
    out_shape = [jax.ShapeDtypeStruct((n, c), F32) for c in widths]
    out_specs = [pl.BlockSpec((tm, c), row) for c in widths]
    if prompt:
        out_shape += [jax.ShapeDtypeStruct((n, HALF + LANES), BF), jax.ShapeDtypeStruct((n, HALF), BF),
                      jax.ShapeDtypeStruct((batch, seq // tm, HALF, tm), BF),
                      jax.ShapeDtypeStruct((batch, seq // LANES, HALF, LANES), BF),
                      jax.ShapeDtypeStruct((n // CMP_BLOCK, KVW), F32)]
        out_specs += [pl.BlockSpec((tm, HALF + LANES), row), pl.BlockSpec((tm, HALF), row),
                      pl.BlockSpec((1, 1, HALF, tm), lambda i: (i // per_seq, i % per_seq, 0, 0)),
                      pl.BlockSpec((1, tm // LANES, HALF, LANES), lambda i: (i // per_seq, i % per_seq, 0, 0)),
                      pl.BlockSpec((tm // CMP_BLOCK, KVW), row)]
    return pl.pallas_call(
        functools.partial(_proj_body, prompt=prompt, seq=seq),
        grid=(steps,),
        in_specs=[pl.BlockSpec((tm, d), row), pl.BlockSpec((1, d), fixed),
                  pl.BlockSpec((d, PROJ_PAD), fixed), pl.BlockSpec((CMP_BLOCK, KVW), fixed)],
        out_specs=out_specs,
        out_shape=out_shape,
        compiler_params=_cparams(1),
        name="proj_prompt" if prompt else "proj_sample",
    )(x2, g, w, wc)


def _select_blocks(imp, forced, visible, n_pick):
    nb = imp.shape[0]
    blk = lax.broadcasted_iota(jnp.int32, imp.shape, 0)
    score = jnp.where(forced, -jnp.inf, jnp.where(visible, imp, -1.0))
    sel = forced
    for _ in range(n_pick):
        m = jnp.max(score, axis=0, keepdims=True)
        first = jnp.min(jnp.where(score == m, blk, nb), axis=0, keepdims=True)
        hit = blk == first
        sel = sel | hit
        score = jnp.where(hit, -jnp.inf, score)
    return sel


def _nsa_prompt_body(q_ref, misc_ref, nz_ref, cmp_ref, ks_ref, vst_ref, kw_ref, vwt_ref, out_ref, *, seq):
    tq = NSA_TQ
    tk = PROJ_TM
    nb = seq // SEL_BLOCK
    wl = NSA_G * tq
    n_piece = WINDOW // LANES + 1
    i = pl.program_id(1)
    q0 = i * tq
    qf = q_ref[...] * (HEAD_DIM ** -0.5 * LOG2E)
    gate_t = _sigmoid(misc_ref[...]).T
    lane_head = lax.broadcasted_iota(jnp.int32, (tq, LANES), 1) // HEAD_DIM
    kc = cmp_ref[0]
    kcb = kc[:, :HALF].astype(BF)
    vct = kc[:, HALF:].T.astype(BF)

    ql = lax.broadcasted_iota(jnp.int32, (1, wl), 1) % tq
    qpos = q0 + ql
    blk_w = lax.broadcasted_iota(jnp.int32, (nb, wl), 0)
    vis_c = blk_w * CMP_BLOCK + (CMP_BLOCK - 1) <= qpos
    blk1 = lax.broadcasted_iota(jnp.int32, (nb, tq), 0)
    cur = (q0 + lax.broadcasted_iota(jnp.int32, (nb, tq), 1)) // SEL_BLOCK
    forced = (blk1 == 0) | (blk1 == cur) | (blk1 == cur - 1)
    visible = blk1 <= cur
    row_p = lax.broadcasted_iota(jnp.int32, (LANES, wl), 0)
    near_ok = row_p > ql
    diag_ok = row_p <= ql

    qs_h, qa_h, oc_h = [], [], []
    for h in range(NSA_KV):
        tiles = []
        for g in range(NSA_G):
            c = h * NSA_G + g
            t = qf[:, (c // 2) * LANES:(c // 2 + 1) * LANES]
            if c % 2 != h:
                t = pltpu.roll(t, HEAD_DIM, axis=1)
            tiles.append(jnp.where(lane_head == h, t, 0.0))
        qs = jnp.concatenate(tiles, axis=0).astype(BF)

        s = jnp.where(vis_c, _dot_nt(kcb, qs), NEG)
        m = jnp.max(s, axis=0, keepdims=True)
        e = jnp.where(vis_c, jnp.exp2(s - m), 0.0)
        l = jnp.sum(e, axis=0, keepdims=True)
        p_c = e / jnp.where(l > 0.0, l, 1.0)
        oc_h.append(_dot(vct[h * HEAD_DIM:(h + 1) * HEAD_DIM], p_c.astype(BF)))

        imp = p_c[:, 0:tq]
        for g in range(1, NSA_G):
            imp = imp + p_c[:, g * tq:(g + 1) * tq]
        sel = _select_blocks(imp, forced, visible, TOP_K - 3)
        bias = jnp.where(sel, 0.0, NEG)
        if nb < LANES:
            bias = jnp.concatenate([bias, jnp.zeros((LANES - nb, tq), F32)], axis=0)
        bias_t = bias.T.astype(BF)
        qs_h.append(qs)
        qa_h.append(jnp.concatenate([qs, jnp.concatenate([bias_t] * NSA_G, axis=0)], axis=1))

    def chunk(c, carry, causal):
        k0 = pl.multiple_of(c * tk, tk)
        ka = ks_ref[pl.ds(k0, tk), :]
        scores = [_dot_nt(ka, qa_h[h]) for h in range(NSA_KV)]
        stats, probs = [], []
        for h in range(NSA_KV):
            m_i, l_i, _ = carry[h]
            s = scores[h]
            if causal:
                kpos = k0 + lax.broadcasted_iota(jnp.int32, (tk, wl), 0)
                s = jnp.where(kpos <= qpos, s, NEG)
            m_n = jnp.maximum(m_i, jnp.max(s, axis=0, keepdims=True))
            alpha = jnp.exp2(m_i - m_n)
            p = jnp.exp2(s - m_n)
            stats.append((m_n, alpha * l_i + jnp.sum(p, axis=0, keepdims=True), alpha))
            probs.append(p.astype(BF))
        out = []
        for h in range(NSA_KV):
            m_n, l_n, alpha = stats[h]
            vt = vst_ref[0, c, h * HEAD_DIM:(h + 1) * HEAD_DIM, :]
            out.append((m_n, l_n, alpha * carry[h][2] + _dot(vt, probs[h])))
        return tuple(out)

    init = (jnp.full((1, wl), NEG, F32), jnp.zeros((1, wl), F32), jnp.zeros((HEAD_DIM, wl), F32))
    n_full = q0 // tk
    carry = lax.fori_loop(0, n_full, functools.partial(chunk, causal=False), (init,) * NSA_KV)
    carry = chunk(n_full, carry, True)

    for h in range(NSA_KV):
        m_s, l_s, acc_s = carry[h]
        o_s = acc_s / l_s
        s_parts = []
        for j in range(n_piece):
            start = q0 - WINDOW + j * LANES
            cl = pl.multiple_of(jnp.maximum(start, 0), LANES)
            sw = _dot_nt(kw_ref[pl.ds(cl, LANES), :], qs_h[h])
            if j == 0:
                sw = jnp.where(near_ok, sw, NEG)
            if j == n_piece - 1:
                sw = jnp.where(diag_ok, sw, NEG)
            else:
                sw = jnp.where(start >= 0, sw, NEG)
            s_parts.append(sw)
        sw = jnp.concatenate(s_parts, axis=0)
        mw = jnp.max(sw, axis=0, keepdims=True)
        pw = jnp.exp2(sw - mw)
        lw = jnp.sum(pw, axis=0, keepdims=True)
        pwb = pw.astype(BF)
        o_w = jnp.zeros((HEAD_DIM, wl), F32)
        for j in range(n_piece):
            cj = jnp.maximum(i - WINDOW // LANES + j, 0)
            vt = vwt_ref[0, cj, h * HEAD_DIM:(h + 1) * HEAD_DIM, :]
            o_w = o_w + _dot(vt, pwb[j * LANES:(j + 1) * LANES])
        o_w = o_w / lw

        def gate_row(j):
            return jnp.concatenate(
                [gate_t[(h * NSA_G + g) * 3 + j:(h * NSA_G + g) * 3 + j + 1, :] for g in range(NSA_G)], axis=1)

        o_t = gate_row(0) * oc_h[h] + gate_row(1) * o_s + gate_row(2) * o_w
        for gp in range(NSA_G // 2):
            pair = jnp.concatenate([o_t[:, (2 * gp) * tq:(2 * gp + 1) * tq],
                                    o_t[:, (2 * gp + 1) * tq:(2 * gp + 2) * tq]], axis=0)
            c0 = (h * (NSA_G // 2) + gp) * LANES
            out_ref[:, c0:c0 + LANES] = pair.T * _silu(nz_ref[:, c0:c0 + LANES])


def _nsa_prompt(q, misc, nz, cmp, ksa, vst, kwb, vwt, *, batch, seq):
    tq = NSA_TQ
    nq = seq // tq
    nb = seq // SEL_BLOCK
    assert tq == LANES and WINDOW % LANES == 0 and nb <= LANES
    row = lambda b, i: (b * nq + i, 0)
    return pl.pallas_call(
        functools.partial(_nsa_prompt_body, seq=seq),
        grid=(batch, nq),
        in_specs=[pl.BlockSpec((tq, NSA_WIDTH), row), pl.BlockSpec((tq, LANES), row),
                  pl.BlockSpec((tq, NSA_WIDTH), row),
                  pl.BlockSpec((1, nb, KVW), lambda b, i: (b, 0, 0)),
                  pl.BlockSpec((seq, HALF + LANES), lambda b, i: (b, 0)),
                  pl.BlockSpec((1, seq // PROJ_TM, HALF, PROJ_TM), lambda b, i: (b, 0, 0, 0)),
                  pl.BlockSpec((seq, HALF), lambda b, i: (b, 0)),
                  pl.BlockSpec((1, seq // LANES, HALF, LANES), lambda b, i: (b, 0, 0, 0))],
        out_specs=pl.BlockSpec((tq, NSA_WIDTH), row),
        out_shape=jax.ShapeDtypeStruct((batch * seq, NSA_WIDTH), F32),
        compiler_params=_cparams(2),
        name="nsa_prompt",
    )(q, misc, nz, cmp.reshape(batch, nb, KVW), ksa, vst, kwb, vwt)


def _cmp_past_body(pt_ref, *refs):
    del pt_ref
    pages, wc_ref, out_ref = refs[:-2], refs[-2], refs[-1]
    rows = jnp.concatenate([pg[0, 0].T for pg in pages], axis=0)
    blocks = rows.reshape(rows.shape[0] // CMP_BLOCK, CMP_BLOCK, KVW)
    out_ref[0] = jnp.sum(blocks * wc_ref[...][None], axis=1)


def _page_spec(layer, page_size, r, pps):
    return pl.BlockSpec((1, 1, KVW, page_size), lambda b, c, pt: (layer, pt[b, c * pps + r], 0, 0))


def _rows_minor(cache):
    n_layers, n_pool, n_rows = cache.shape[:3]
    return jnp.transpose(cache, (0, 1, 3, 4, 5, 2)).reshape(n_layers, n_pool, KVW, n_rows)


def _cmp_past(cache4, page_table, wc, *, layer):
    page_size = cache4.shape[3]
    bd, n_pages = page_table.shape
    pps = PAGES_PER_STEP
    per_page = page_size // CMP_BLOCK
    grid_spec = pltpu.PrefetchScalarGridSpec(
        num_scalar_prefetch=1,
        grid=(bd, n_pages // pps),
        in_specs=[_page_spec(layer, page_size, r, pps) for r in range(pps)]
        + [pl.BlockSpec((CMP_BLOCK, KVW), lambda b, c, pt: (0, 0))],
        out_specs=pl.BlockSpec((1, pps * per_page, KVW), lambda b, c, pt: (b, c, 0)),
    )
    return pl.pallas_call(
        _cmp_past_body,
        grid_spec=grid_spec,
        out_shape=jax.ShapeDtypeStruct((bd, n_pages * per_page, KVW), F32),
        compiler_params=_cparams(2),
        name="cmp_past",
    )(page_table, *([cache4] * pps), wc)


def _nsa_sample_body(pt_ref, *refs, t_new, past_len, page_size):
    del pt_ref
    pps = PAGES_PER_STEP
    pages = refs[:pps]
    (q_ref, misc_ref, nz_ref, cmp_ref, kvs_ref, kvw_ref, win_ref, out_ref, newwin_ref,
     qs_ref, masked_ref, m_ref, l_ref, acc_ref, expand_ref) = refs[pps:]
    c = pl.program_id(1)
    n_chunks = pl.num_programs(1)
    used = NSA_HEADS * t_new
    rows = LANES
    nbp = past_len // SEL_BLOCK
    keys = pps * page_size
    blk_per_step = keys // SEL_BLOCK
    t_of_row = lax.broadcasted_iota(jnp.int32, (rows, 1), 0) % t_new

    @pl.when((pl.program_id(0) == 0) & (c == 0))
    def _():
        eb = lax.broadcasted_iota(jnp.int32, (LANES, keys), 0)
        ek = lax.broadcasted_iota(jnp.int32, (LANES, keys), 1) // SEL_BLOCK
        expand_ref[...] = jnp.where(eb == ek, NEG, 0.0).astype(BF)

    @pl.when(c == 0)
    def _():
        qf = q_ref[...] * (HEAD_DIM ** -0.5)
        lane_head = lax.broadcasted_iota(jnp.int32, (t_new, LANES), 1) // HEAD_DIM
        tiles = []
        for hg in range(NSA_HEADS):
            h = hg // NSA_G
            t = qf[:, (hg // 2) * LANES:(hg // 2 + 1) * LANES]
            if hg % 2 != h:
                t = pltpu.roll(t, HEAD_DIM, axis=1)
            tiles.append(jnp.where(lane_head == h, t, 0.0))
        tiles.append(jnp.zeros((rows - used, LANES), F32))
        qs = jnp.concatenate(tiles, axis=0)
        qs_ref[...] = qs
        kc = cmp_ref[0]
        s = _dot_nt(kc[:, :HALF].astype(BF), qs.astype(BF))
        m = jnp.max(s, axis=0, keepdims=True)
        e = jnp.exp(s - m)
        p_c = e / jnp.sum(e, axis=0, keepdims=True)
        o_c = _dot(kc[:, HALF:].T.astype(BF), p_c.astype(BF))
        acc_ref[1] = o_c.T
        ri = lax.broadcasted_iota(jnp.int32, (rows, rows), 0)
        ci = lax.broadcasted_iota(jnp.int32, (rows, rows), 1)
        same = ((ri // (NSA_G * t_new)) == (ci // (NSA_G * t_new))) & ((ri % t_new) == (ci % t_new))
        gsum = jnp.where(same, 1.0, 0.0).astype(BF)
        p0, p1, p2 = _split3(p_c)
        imp = _dot(p0, gsum) + _dot(p1, gsum) + _dot(p2, gsum)
        blk = lax.broadcasted_iota(jnp.int32, (nbp, rows), 0)
        forced = (blk == 0) | (blk == nbp - 1)
        sel = _select_blocks(imp, forced, blk >= 0, TOP_K - 3)
        masked_ref[...] = jnp.where(sel, 0.0, 1.0)
        m_ref[...] = jnp.full(m_ref.shape, NEG, F32)
        l_ref[...] = jnp.zeros(l_ref.shape, F32)
        acc_ref[0] = jnp.zeros(acc_ref.shape[1:], F32)

    qsb = qs_ref[...].astype(BF)
    b0 = pl.multiple_of(c * blk_per_step, blk_per_step)
    step_mask = jnp.concatenate([masked_ref[pl.ds(b0, blk_per_step), :],
                                 jnp.zeros((LANES - blk_per_step, rows), F32)], axis=0)
    bias = _dot(step_mask.T.astype(BF), expand_ref[...])
    s = jnp.concatenate([_dot(qsb, pg[0, 0, :HALF, :].astype(BF)) for pg in pages], axis=1) + bias
    m_i = m_ref[...]
    m_n = jnp.maximum(m_i, jnp.max(s, axis=1, keepdims=True))
    alpha = jnp.exp(m_i - m_n)
    p = jnp.exp(s - m_n)
    l_ref[...] = alpha * l_ref[...] + jnp.sum(p, axis=1, keepdims=True)
    pb = p.astype(BF)
    acc = alpha * acc_ref[0]
    for r, pg in enumerate(pages):
        acc = acc + _dot_nt(pb[:, r * page_size:(r + 1) * page_size], pg[0, 0, HALF:, :].astype(BF))
    acc_ref[0] = acc
    m_ref[...] = m_n

    @pl.when(c == n_chunks - 1)
    def _():
        kvs = kvs_ref[...]
        tn = lax.broadcasted_iota(jnp.int32, (rows, t_new), 1)
        s_n = jnp.where(tn <= t_of_row, _dot_nt(qsb, kvs[:, :HALF].astype(BF)), NEG)
        m_i = m_ref[...]
        m_n = jnp.maximum(m_i, jnp.max(s_n, axis=1, keepdims=True))
        alpha = jnp.exp(m_i - m_n)
        p_n = jnp.exp(s_n - m_n)
        l_s = alpha * l_ref[...] + jnp.sum(p_n, axis=1, keepdims=True)
        o_s = (alpha * acc_ref[0] + _dot(p_n.astype(BF), kvs[:, HALF:].astype(BF))) / l_s
        wbt = win_ref[0, 0]
        w_buf = wbt.shape[1]
        kvw = kvw_ref[...]
        rb = lax.broadcasted_iota(jnp.int32, (rows, w_buf), 1)
        s_b = jnp.where(w_buf + t_of_row - rb < WINDOW, _dot(qsb, wbt[:HALF].astype(BF)), NEG)
        s_w = jnp.where(tn <= t_of_row, _dot_nt(qsb, kvw[:, :HALF].astype(BF)), NEG)
        m_w = jnp.maximum(jnp.max(s_b, axis=1, keepdims=True), jnp.max(s_w, axis=1, keepdims=True))
        p_b = jnp.exp(s_b - m_w)
        p_w = jnp.exp(s_w - m_w)
        l_w = jnp.sum(p_b, axis=1, keepdims=True) + jnp.sum(p_w, axis=1, keepdims=True)
        o_w = (_dot_nt(p_b.astype(BF), wbt[HALF:].astype(BF)) + _dot(p_w.astype(BF), kvw[:, HALF:].astype(BF))) / l_w
        shifted = pltpu.roll(wbt, w_buf - t_new, axis=1)
        newwin_ref[0, :, :w_buf - LANES] = shifted[:, :w_buf - LANES]
        tail = jnp.concatenate([jnp.zeros((LANES - t_new, KVW), F32), kvw], axis=0).T
        lane = lax.broadcasted_iota(jnp.int32, (KVW, LANES), 1)
        newwin_ref[0, :, w_buf - LANES:] = jnp.where(lane >= LANES - t_new, tail, shifted[:, w_buf - LANES:])
        gates = _sigmoid(misc_ref[...])
        o_c = acc_ref[1]
        nz = nz_ref[...]
        for hg in range(NSA_HEADS):
            h = hg // NSA_G
            r0 = hg * t_new
            o = (gates[:, hg * 3:hg * 3 + 1] * o_c[r0:r0 + t_new]
                 + gates[:, hg * 3 + 1:hg * 3 + 2] * o_s[r0:r0 + t_new]
                 + gates[:, hg * 3 + 2:hg * 3 + 3] * o_w[r0:r0 + t_new])
            c0 = hg * HEAD_DIM
            out_ref[:, c0:c0 + HEAD_DIM] = o[:, h * HEAD_DIM:(h + 1) * HEAD_DIM] * _silu(nz[:, c0:c0 + HEAD_DIM])


def _nsa_sample(q, misc, nz, cmp_past, kvs, kvw, cache4, win4, page_table, *, layer, t_new):
    page_size = cache4.shape[3]
    bd, n_pages = page_table.shape
    past_len = n_pages * page_size
    w_buf = win4.shape[3]
    pps = PAGES_PER_STEP
    nbp = past_len // SEL_BLOCK
    row = lambda b, c, pt: (b, 0)
    grid_spec = pltpu.PrefetchScalarGridSpec(
        num_scalar_prefetch=1,
        grid=(bd, n_pages // pps),
        in_specs=[_page_spec(layer, page_size, r, pps) for r in range(pps)]
        + [pl.BlockSpec((t_new, NSA_WIDTH), row), pl.BlockSpec((t_new, LANES), row),
           pl.BlockSpec((t_new, NSA_WIDTH), row),
           pl.BlockSpec((1, nbp, KVW), lambda b, c, pt: (b, 0, 0)),
           pl.BlockSpec((t_new, KVW), row), pl.BlockSpec((t_new, KVW), row),
           pl.BlockSpec((1, 1, KVW, w_buf), lambda b, c, pt: (layer, b, 0, 0))],
        out_specs=[pl.BlockSpec((t_new, NSA_WIDTH), row),
                   pl.BlockSpec((1, KVW, w_buf), lambda b, c, pt: (b, 0, 0))],
        scratch_shapes=[pltpu.VMEM((LANES, LANES), F32), pltpu.VMEM((nbp, LANES), F32),
                        pltpu.VMEM((LANES, 1), F32), pltpu.VMEM((LANES, 1), F32),
                        pltpu.VMEM((2, LANES, LANES), F32), pltpu.VMEM((LANES, pps * page_size), BF)],
    )
    return pl.pallas_call(
        functools.partial(_nsa_sample_body, t_new=t_new, past_len=past_len, page_size=page_size),
        grid_spec=grid_spec,
        out_shape=[jax.ShapeDtypeStruct((bd * t_new, NSA_WIDTH), F32),
                   jax.ShapeDtypeStruct((bd, KVW, w_buf), F32)],
        compiler_params=_cparams(2),
        name="nsa_sample",
    )(page_table, *([cache4] * pps), q, misc, nz, cmp_past, kvs, kvw, win4)


def _gla_body(gla_ref, misc_ref, a2_ref, ab_ref, gn_ref, s0_ref, out_ref, sout_ref, st_ref, *, chunk, mxu_dtype):
    j = pl.program_id(1)
    n_steps = pl.num_programs(1)
    tb = gla_ref.shape[0]
    sub = min(GLA_SUB, chunk)
    hk = GLA_HEADS * chunk

    @pl.when(j == 0)
    def _():
        s0 = s0_ref[0].reshape(GLA_KW, GLA_DV)
        ri = lax.broadcasted_iota(jnp.int32, (GLA_WIDTH, GLA_DV), 0) % GLA_DV
        ci = lax.broadcasted_iota(jnp.int32, (GLA_WIDTH, GLA_DV), 1)
        pick = jnp.where(ri == ci, 1.0, 0.0).astype(BF)
        wide = sum(_dot_nt(pick, part) for part in _split3(s0))
        st_ref[...] = jnp.where(_gla_diag(GLA_WIDTH, GLA_DV, GLA_KW, GLA_DK), wide, 0.0)

    diag_sv = _gla_diag(GLA_WIDTH, GLA_DV, GLA_KW, GLA_DK)
    diag_k = _gla_diag(hk, chunk, GLA_KW, GLA_DK)
    diag_v = _gla_diag(hk, chunk, GLA_WIDTH, GLA_DV)
    tril = jnp.where(lax.broadcasted_iota(jnp.int32, (chunk, chunk), 0)
                     >= lax.broadcasted_iota(jnp.int32, (chunk, chunk), 1), 1.0, 0.0).astype(BF)
    srow = lax.broadcasted_iota(jnp.int32, (chunk, GLA_KW), 0)
    ri = lax.broadcasted_iota(jnp.int32, (GLA_WIDTH, GLA_WIDTH), 0) // GLA_DV
    ci = lax.broadcasted_iota(jnp.int32, (GLA_WIDTH, GLA_WIDTH), 1) // GLA_DV
    head_mean = jnp.where(ri == ci, 1.0 / GLA_DV, 0.0).astype(BF)
    a_col = lax.broadcasted_iota(jnp.int32, (sub, hk), 1) % chunk
    a_row = lax.broadcasted_iota(jnp.int32, (sub, hk), 0)

    def cast(a):
        return a.astype(mxu_dtype)

    def one_chunk(ci_, carry):
        r0 = pl.multiple_of(ci_ * chunk, chunk)
        blk = gla_ref[pl.ds(r0, chunk), :]
        gq = blk[:, 0:GLA_KW] * (GLA_DK ** -0.5)
        gk = blk[:, GLA_KW:2 * GLA_KW]
        gv = blk[:, 2 * GLA_KW:2 * GLA_KW + GLA_WIDTH]
        gz = blk[:, 2 * GLA_KW + GLA_WIDTH:]
        a_logit = _dot(cast(misc_ref[pl.ds(r0, chunk), :]), cast(a2_ref[...])) + ab_ref[...]
        log_a = (jnp.minimum(a_logit, 0.0) - jnp.log(1.0 + jnp.exp(-jnp.abs(a_logit)))) * (1.0 / GLA_TAU)
        la0, la1, la2 = _split3(log_a)
        bcum = _dot(tril, la0) + _dot(tril, la1) + _dot(tril, la2)
        st = st_ref[...]
        inter = _dot_nt(cast(gq * jnp.exp(bcum)), cast(st))
        v_bd = jnp.where(diag_v, jnp.concatenate([gv] * GLA_HEADS, axis=0), 0.0)
        v_bd = cast(v_bd)
        parts = []
        for sb in range(chunk // sub):
            t0 = sb * sub
            ref_row = bcum[t0:t0 + 1, :]
            qd = gq[t0:t0 + sub] * jnp.exp(bcum[t0:t0 + sub] - ref_row)
            kd = gk * jnp.exp(jnp.where(srow < t0 + sub, ref_row - bcum, NEG))
            k_bd = jnp.where(diag_k, jnp.concatenate([kd] * GLA_HEADS, axis=0), 0.0)
            att = _dot_nt(cast(qd), cast(k_bd))
            att = jnp.where(a_col <= a_row + t0, att, 0.0)
            parts.append(_dot(cast(att), v_bd))
        o = inter + jnp.concatenate(parts, axis=0)
        b_last = bcum[chunk - 1:chunk, :]
        kdec = gk * jnp.exp(b_last - bcum)
        upd = _dot(cast(_transpose_mxu(gv, mxu_dtype)), cast(kdec))
        st_ref[...] = st * jnp.exp(b_last) + jnp.where(diag_sv, upd, 0.0)
        o2 = o * o
        q0, q1, _ = _split3(o2)
        ms = _dot(q0, head_mean) + _dot(q1, head_mean)
        out_ref[pl.ds(r0, chunk), :] = o * lax.rsqrt(ms + EPS) * gn_ref[...] * _silu(gz)
        return carry

    lax.fori_loop(0, tb // chunk, one_chunk, 0)

    @pl.when(j == n_steps - 1)
    def _():
        st = st_ref[...]
        acc = st[0:GLA_DV]
        for h in range(1, GLA_HEADS):
            acc = acc + st[h * GLA_DV:(h + 1) * GLA_DV]
        sout_ref[0] = _transpose_mxu(acc, F32).reshape(GLA_HEADS, GLA_DK, GLA_DV)


def _gla_diag(n_rows, row_group, n_cols, col_group):
    r = lax.broadcasted_iota(jnp.int32, (n_rows, n_cols), 0) // row_group
    c = lax.broadcasted_iota(jnp.int32, (n_rows, n_cols), 1) // col_group
    return r == c


def _gla(gla, misc, a2p, ab, gn, s0, *, batch, seq):
    chunk = min(GLA_CHUNK, seq)
    tb = min(PROJ_TM, seq)
    steps = seq // tb
    row = lambda b, j: (b * steps + j, 0)
    fixed = lambda b, j: (0, 0)
    return pl.pallas_call(
        functools.partial(_gla_body, chunk=chunk, mxu_dtype=BF if chunk >= 16 else F32),
        grid=(batch, steps),
        in_specs=[pl.BlockSpec((tb, W_GLA), row), pl.BlockSpec((tb, LANES), row),
                  pl.BlockSpec((LANES, GLA_KW), fixed), pl.BlockSpec((1, GLA_KW), fixed),
                  pl.BlockSpec((1, GLA_WIDTH), fixed),
                  pl.BlockSpec((1, GLA_HEADS, GLA_DK, GLA_DV), lambda b, j: (b, 0, 0, 0))],
        out_specs=[pl.BlockSpec((tb, GLA_WIDTH), row),
                   pl.BlockSpec((1, GLA_HEADS, GLA_DK, GLA_DV), lambda b, j: (b, 0, 0, 0))],
        out_shape=[jax.ShapeDtypeStruct((batch * seq, GLA_WIDTH), F32),
                   jax.ShapeDtypeStruct((batch, GLA_HEADS, GLA_DK, GLA_DV), F32)],
        scratch_shapes=[pltpu.VMEM((GLA_WIDTH, GLA_KW), F32)],
        compiler_params=_cparams(2),
        name="gla_prompt" if seq > GLA_CHUNK else "gla_sample",
    )(gla, misc, a2p, ab, gn, s0)


def _out_body(x_ref, nsa_ref, gla_ref, conv_ref, halo_ref, buf_ref, cw_ref, wo_ref, g_ref, y_ref, nbuf_ref,
              *, seq, tm):
    cv = conv_ref[...]
    cb = cv[:, 0:CONV_WIDTH]
    u = cv[:, CONV_WIDTH:2 * CONV_WIDTH] * cv[:, 2 * CONV_WIDTH:3 * CONV_WIDTH]
    cz = cv[:, 3 * CONV_WIDTH:]
    rows = lax.broadcasted_iota(jnp.int32, (tm, CONV_WIDTH), 0)
    if seq >= tm:
        first = pl.program_id(0) % (seq // tm) == 0
        hv = halo_ref[...]
        hu = hv[:, CONV_WIDTH:2 * CONV_WIDTH] * hv[:, 2 * CONV_WIDTH:3 * CONV_WIDTH]
        bufv = buf_ref[0]
        p1 = jnp.where(first, bufv[1:2], hu[7:8])
        p2 = jnp.where(first, bufv[0:1], hu[6:7])
        prev1 = jnp.where(rows == 0, p1, pltpu.roll(u, 1, axis=0))
        prev2 = jnp.where(rows == 0, p2, jnp.where(rows == 1, p1, pltpu.roll(u, 2, axis=0)))
        nbuf_ref[0] = u[tm - (CONV_K - 1):, :]
    else:
        nseq = tm // seq
        bufv = buf_ref[...]
        b0 = jnp.broadcast_to(bufv[:, 0:1, :], (nseq, seq, CONV_WIDTH)).reshape(tm, CONV_WIDTH)
        b1 = jnp.broadcast_to(bufv[:, 1:2, :], (nseq, seq, CONV_WIDTH)).reshape(tm, CONV_WIDTH)
        t = rows % seq
        prev1 = jnp.where(t == 0, b1, pltpu.roll(u, 1, axis=0))
        prev2 = jnp.where(t == 0, b0, jnp.where(t == 1, b1, pltpu.roll(u, 2, axis=0)))
        nbuf_ref[...] = u.reshape(nseq, seq, CONV_WIDTH)[:, seq - (CONV_K - 1):, :]
    cw = cw_ref[...]
    y = prev2 * cw[0:1] + prev1 * cw[1:2] + u * cw[2:3]
    o_conv = cb * y * _silu(cz)
    acc = _dot(nsa_ref[...].astype(BF), wo_ref[0:NSA_WIDTH, :])
    acc = acc + _dot(gla_ref[...].astype(BF), wo_ref[NSA_WIDTH:NSA_WIDTH + GLA_WIDTH, :])
    acc = acc + _dot(o_conv.astype(BF), wo_ref[NSA_WIDTH + GLA_WIDTH:, :])
    normed = acc * lax.rsqrt(jnp.mean(acc * acc, axis=-1, keepdims=True) + EPS) * g_ref[...]
    y_ref[...] = x_ref[...] + normed


def _out(x2, o_nsa, o_gla, conv, buf, cw, wo, g, *, batch, seq):
    n, d = x2.shape
    tm = PROJ_TM if seq >= PROJ_TM else n
    steps = n // tm
    row = lambda i: (i, 0)
    fixed = lambda i: (0, 0)
    if seq >= tm:
        per_seq = seq // tm
        halo_spec = pl.BlockSpec((8, W_CONV), lambda i: (jnp.maximum(i * (tm // 8) - 1, 0), 0))
        buf_spec = pl.BlockSpec((1, CONV_K - 1, CONV_WIDTH), lambda i: (i // per_seq, 0, 0))
    else:
        halo_spec = pl.BlockSpec((8, W_CONV), fixed)
        buf_spec = pl.BlockSpec((batch, CONV_K - 1, CONV_WIDTH), lambda i: (0, 0, 0))
    return pl.pallas_call(
        functools.partial(_out_body, seq=seq, tm=tm),
        grid=(steps,),
        in_specs=[pl.BlockSpec((tm, d), row), pl.BlockSpec((tm, NSA_WIDTH), row),
                  pl.BlockSpec((tm, GLA_WIDTH), row), pl.BlockSpec((tm, W_CONV), row),
                  halo_spec, buf_spec,
                  pl.BlockSpec((CONV_K, CONV_WIDTH), fixed), pl.BlockSpec((d, d), fixed),
                  pl.BlockSpec((1, d), fixed)],
        out_specs=[pl.BlockSpec((tm, d), row), buf_spec],
        out_shape=[jax.ShapeDtypeStruct((n, d), F32),
                   jax.ShapeDtypeStruct((batch, CONV_K - 1, CONV_WIDTH), F32)],
        compiler_params=_cparams(1),
        name="out_prompt" if seq >= PROJ_TM else "out_sample",
    )(x2, o_nsa, o_gla, conv, conv, buf, cw, wo, g)


def _reorder_w_in(w):
    o_gate = NSA_WIDTH + 3 * KVW
    o_nz = o_gate + N_GATE
    o_gla = o_nz + NSA_WIDTH
    o_a = o_gla + 2 * GLA_KW + GLA_WIDTH
    o_gz = o_a + GLA_RANK
    o_conv = o_gz + GLA_WIDTH
    pad = jnp.zeros((w.shape[0], LANES - N_GATE - GLA_RANK), w.dtype)
    return jnp.concatenate([w[:, :o_gate], w[:, o_nz:o_gla], w[:, o_gla:o_a], w[:, o_gz:o_conv], w[:, o_conv:],
                            w[:, o_gate:o_nz], w[:, o_a:o_gz], pad], axis=1)


def _layer(x, l, P, *, prompt, cache_cmp_kv=None, cache_slc_kv=None, cache_win_kv=None, page_table=None,
           gla_s0=None, conv_buf=None):
    batch, seq, d = x.shape
    x2 = x.reshape(batch * seq, d)
    outs = _proj(x2, P['norm_pre'][l], P['w_in'][l], P['w_cmp'][l], batch=batch, seq=seq, prompt=prompt)
    q, kvc, kvs, kvw, nz, gla, conv, misc = outs[:8]
    kv_shape = (batch, seq, 2, NSA_KV, HEAD_DIM)
    if prompt:
        ksb, kwb, vst, vwt, cmp = outs[8:]
        o_nsa = _nsa_prompt(q, misc, nz, cmp, ksb, vst, kwb, vwt, batch=batch, seq=seq)
        w_keep = min(WINDOW, seq)
        new_w = kvw.reshape(kv_shape)[:, seq - w_keep:]
    else:
        cmp_past = _cmp_past(cache_cmp_kv, page_table, P['w_cmp'][l], layer=l)
        o_nsa, new_w = _nsa_sample(q, misc, nz, cmp_past, kvs, kvw, cache_slc_kv, cache_win_kv, page_table,
                                   layer=l, t_new=seq)
        w_buf = new_w.shape[2]
        new_w = jnp.transpose(new_w.reshape((batch,) + kv_shape[2:] + (w_buf,)), (0, 4, 1, 2, 3))
    o_gla, s_gla = _gla(gla, misc, P['gla_a2'][l], P['gla_ab'][l], P['gla_norm'][l], gla_s0, batch=batch, seq=seq)
    y2, new_buf = _out(x2, o_nsa, o_gla, conv, conv_buf, P['conv_w'][l], P['w_out'][l], P['norm_post'][l],
                       batch=batch, seq=seq)
    return y2.reshape(batch, seq, d), (kvc.reshape(kv_shape), kvs.reshape(kv_shape), new_w, s_gla, new_buf)


def kernel(x_prompt, x_sample, cache_cmp_kv, cache_slc_kv, cache_win_kv, state_gla, state_conv, page_table,
           norm_pre, norm_post, w_in, w_out, w_cmp_k, w_cmp_v, gla_a2, gla_ab, gla_norm, conv_w):
    depth = w_in.shape[0]
    d = w_in.shape[1]
    ones = jnp.ones((1, HALF), F32)
    a2p = jnp.zeros((depth, LANES, GLA_KW), F32).at[:, N_GATE:N_GATE + GLA_RANK, :].set(gla_a2)
    P = {
        'norm_pre': norm_pre.reshape(depth, 1, d),
        'norm_post': norm_post.reshape(depth, 1, d),
        'w_in': jax.vmap(_reorder_w_in)(w_in).astype(BF),
        'w_out': w_out.astype(BF),
        'w_cmp': jnp.concatenate([w_cmp_k[:, :, None] * ones, w_cmp_v[:, :, None] * ones], axis=2),
        'gla_a2': a2p,
        'gla_ab': gla_ab.reshape(depth, 1, GLA_KW),
        'gla_norm': jnp.tile(gla_norm, (1, GLA_HEADS)).reshape(depth, 1, GLA_WIDTH),
        'conv_w': conv_w,
    }
    xp, xs = x_prompt, x_sample
    bp = xp.shape[0]
    st_p, st_s = [], []
    cache_cmp_kv, cache_slc_kv, cache_win_kv = (_rows_minor(c) for c in (cache_cmp_kv, cache_slc_kv, cache_win_kv))
    for l in range(depth):
        xp, sp = _layer(xp, l, P, prompt=True,
                        gla_s0=jnp.zeros((bp, GLA_HEADS, GLA_DK, GLA_DV), F32),
                        conv_buf=jnp.zeros((bp, CONV_K - 1, CONV_WIDTH), F32))
        xs, ss = _layer(xs, l, P, prompt=False, cache_cmp_kv=cache_cmp_kv, cache_slc_kv=cache_slc_kv,
                        cache_win_kv=cache_win_kv, page_table=page_table,
                        gla_s0=state_gla[l], conv_buf=state_conv[l])
        st_p.append(sp)
        st_s.append(ss)
    stack = lambda sts, k: jnp.stack([s[k] for s in sts])
    return (xp, xs, stack(st_p, 0), stack(st_s, 0), stack(st_p, 1), stack(st_s, 1), stack(st_p, 2), stack(st_s, 2),
            stack(st_p, 3), stack(st_s, 3), stack(st_p, 4), stack(st_s, 4))
```

```python
import functools

import jax
import jax.numpy as jnp
from jax import lax
from jax.experimental import pallas as pl
from jax.experimental.pallas import tpu as pltpu

F32 = jnp.float32
BF = jnp.bfloat16

HEAD_DIM = 64
NSA_KV = 2
NSA_G = 4
NSA_HEADS = NSA_KV * NSA_G
NSA_WIDTH = NSA_HEADS * HEAD_DIM
KVW = 2 * NSA_KV * HEAD_DIM
HALF = NSA_KV * HEAD_DIM
CMP_BLOCK = 64
SEL_BLOCK = 64
TOP_K = 16
WINDOW = 512
GLA_HEADS = 4
GLA_DK = 32
GLA_DV = 64
GLA_KW = GLA_HEADS * GLA_DK
GLA_WIDTH = GLA_HEADS * GLA_DV
GLA_RANK = 16
GLA_TAU = 16.0
GLA_CHUNK = 64
GLA_SUB = 16
CONV_WIDTH = 256
CONV_K = 3
N_GATE = 3 * NSA_HEADS
EPS = 1e-6
NEG = -1e30
LOG2E = 1.4426950408889634

LANES = 128
VMEM_LIMIT = 48 * 1024 * 1024

C_Q = 0
C_CMP = C_Q + NSA_WIDTH
C_SLC = C_CMP + KVW
C_WIN = C_SLC + KVW
C_NZ = C_WIN + KVW
C_GLA = C_NZ + NSA_WIDTH
W_GLA = 2 * GLA_KW + 2 * GLA_WIDTH
C_CONV = C_GLA + W_GLA
W_CONV = 4 * CONV_WIDTH
C_MISC = C_CONV + W_CONV
PROJ_PAD = C_MISC + LANES

PROJ_TM = 512
NSA_TQ = 128
PAGES_PER_STEP = 32


def _cparams(n_axes, vmem=VMEM_LIMIT):
    return pltpu.CompilerParams(dimension_semantics=("arbitrary",) * n_axes, vmem_limit_bytes=vmem)


def _dot(a, b):
    return jnp.dot(a, b, preferred_element_type=F32)


def _dot_nt(a, b):
    return lax.dot_general(a, b, (((1,), (1,)), ((), ())), preferred_element_type=F32)


def _split3(a):
    a0 = a.astype(BF)
    r = a - a0.astype(F32)
    a1 = r.astype(BF)
    a2 = (r - a1.astype(F32)).astype(BF)
    return a0, a1, a2


def _transpose_mxu(y, dtype):
    n = y.shape[1]
    eye = jnp.where(lax.broadcasted_iota(jnp.int32, (n, n), 0) == lax.broadcasted_iota(jnp.int32, (n, n), 1),
                    1.0, 0.0).astype(BF)
    if dtype == BF:
        return _dot_nt(eye, y.astype(BF))
    return sum(_dot_nt(eye, part) for part in _split3(y))


def _silu(x):
    return x * (1.0 / (1.0 + jnp.exp(-x)))


def _sigmoid(x):
    return 1.0 / (1.0 + jnp.exp(-x))


def _proj_body(x_ref, g_ref, w_ref, wc_ref, *outs, prompt, seq):
    x = x_ref[...]
    h = x * lax.rsqrt(jnp.mean(x * x, axis=-1, keepdims=True) + EPS) * g_ref[...]
    hb = h.astype(BF)

    def proj(c0, n):
        return _dot(hb, w_ref[:, c0:c0 + n])

    q_ref, kvc_ref, kvs_ref, kvw_ref, nz_ref, gla_ref, conv_ref, misc_ref = outs[:8]
    q_ref[...] = proj(C_Q, NSA_WIDTH)
    kvc = proj(C_CMP, KVW)
    kvs = proj(C_SLC, KVW)
    kvw = proj(C_WIN, KVW)
    kvc_ref[...] = kvc
    kvs_ref[...] = kvs
    kvw_ref[...] = kvw
    nz_ref[...] = proj(C_NZ, NSA_WIDTH)
    gla_ref[...] = proj(C_GLA, W_GLA)
    conv_ref[...] = proj(C_CONV, W_CONV)
    misc_ref[...] = proj(C_MISC, LANES)
    if prompt:
        ksb_ref, kwb_ref, vst_ref, vwt_ref, cmp_ref = outs[8:]
        tm = x.shape[0]
        blk = (pl.program_id(0) % (seq // tm)) * (tm // SEL_BLOCK) \
            + lax.broadcasted_iota(jnp.int32, (tm, LANES), 0) // SEL_BLOCK
        onehot = jnp.where(lax.broadcasted_iota(jnp.int32, (tm, LANES), 1) == blk, 1.0, 0.0)
        ksb_ref[...] = jnp.concatenate([kvs[:, :HALF], onehot], axis=1).astype(BF)
        kwb_ref[...] = kvw[:, :HALF].astype(BF)
        vst_ref[0, 0] = kvs[:, HALF:].T.astype(BF)
        for j in range(tm // LANES):
            vwt_ref[0, j] = kvw[j * LANES:(j + 1) * LANES, HALF:].T.astype(BF)
        cmp_ref[...] = jnp.sum(kvc.reshape(tm // CMP_BLOCK, CMP_BLOCK, KVW) * wc_ref[...][None], axis=1)


def _proj(x2, g, w, wc, *, batch, seq, prompt):
    n, d = x2.shape
    tm = PROJ_TM if prompt else n
    steps = n // tm
    per_seq = seq // tm if prompt else 1
    row = lambda i: (i, 0)
    fixed = lambda i: (0, 0)
    widths = [NSA_WIDTH, KVW, KVW, KVW, NSA_WIDTH, W_GLA, W_CONV, LANES]
    out_shape = [jax.ShapeDtypeStruct((n, c), F32) for c in widths]
    out_specs = [pl.BlockSpec((tm, c), row) for c in widths]
    if prompt:
        out_shape += [jax.ShapeDtypeStruct((n, HALF + LANES), BF), jax.ShapeDtypeStruct((n, HALF), BF),
                      jax.ShapeDtypeStruct((batch, seq // tm, HALF, tm), BF),
                      jax.ShapeDtypeStruct((batch, seq // LANES, HALF, LANES), BF),
                      jax.ShapeDtypeStruct((n // CMP_BLOCK, KVW), F32)]
        out_specs += [pl.BlockSpec((tm, HALF + LANES), row), pl.BlockSpec((tm, HALF), row),
                      pl.BlockSpec((1, 1, HALF, tm), lambda i: (i // per_seq, i % per_seq, 0, 0)),
                      pl.BlockSpec((1, tm // LANES, HALF, LANES), lambda i: (i // per_seq, i % per_seq, 0, 0)),
                      pl.BlockSpec((tm // CMP_BLOCK, KVW), row)]
    return pl.pallas_call(
        functools.partial(_proj_body, prompt=prompt, seq=seq),
        grid=(steps,),
        in_specs=[pl.BlockSpec((tm, d), row), pl.BlockSpec((1, d), fixed),
                  pl.BlockSpec((d, PROJ_PAD), fixed), pl.BlockSpec((CMP_BLOCK, KVW), fixed)],
        out_specs=out_specs,
        out_shape=out_shape,
        compiler_params=_cparams(1),
        name="proj_prompt" if prompt else "proj_sample",
    )(x2, g, w, wc)


def _select_blocks(imp, forced, visible, n_pick):
    nb = imp.shape[0]
    blk = lax.broadcasted_iota(jnp.int32, imp.shape, 0)
    score = jnp.where(forced, -jnp.inf, jnp.where(visible, imp, -1.0))
    sel = forced
    for _ in range(n_pick):
        m = jnp.max(score, axis=0, keepdims=True)
        first = jnp.min(jnp.where(score == m, blk, nb), axis=0, keepdims=True)
        hit = blk == first
        sel = sel | hit
        score = jnp.where(hit, -jnp.inf, score)
    return sel


def _nsa_prompt_body(q_ref, misc_ref, nz_ref, cmp_ref, ks_ref, vst_ref, kw_ref, vwt_ref, out_ref, s_ref, *, seq):
    tq = NSA_TQ
    tk = PROJ_TM
    nb = seq // SEL_BLOCK
    n_chunks = seq // tk
    wl = NSA_G * tq
    n_piece = WINDOW // LANES + 1
    i = pl.program_id(1)
    q0 = i * tq
    qf = q_ref[...] * (HEAD_DIM ** -0.5 * LOG2E)
    gate_t = _sigmoid(misc_ref[...]).T
    lane_head = lax.broadcasted_iota(jnp.int32, (tq, LANES), 1) // HEAD_DIM
    kc = cmp_ref[0]
    kcb = kc[:, :HALF].astype(BF)
    vct = kc[:, HALF:].T.astype(BF)

    ql = lax.broadcasted_iota(jnp.int32, (1, wl), 1) % tq
    qpos = q0 + ql
    blk_w = lax.broadcasted_iota(jnp.int32, (nb, wl), 0)
    vis_c = blk_w * CMP_BLOCK + (CMP_BLOCK - 1) <= qpos
    blk2 = lax.broadcasted_iota(jnp.int32, (nb, NSA_KV * tq), 0)
    cur = (q0 + lax.broadcasted_iota(jnp.int32, (nb, NSA_KV * tq), 1) % tq) // SEL_BLOCK
    forced = (blk2 == 0) | (blk2 == cur) | (blk2 == cur - 1)
    visible = blk2 <= cur
    row_p = lax.broadcasted_iota(jnp.int32, (LANES, wl), 0)
    near_ok = row_p > ql
    diag_ok = row_p <= ql

    qs_h, oc_h, imp_h = [], [], []
    for h in range(NSA_KV):
        tiles = []
        for g in range(NSA_G):
            c = h * NSA_G + g
            t = qf[:, (c // 2) * LANES:(c // 2 + 1) * LANES]
            if c % 2 != h:
                t = pltpu.roll(t, HEAD_DIM, axis=1)
            tiles.append(jnp.where(lane_head == h, t, 0.0))
        qs = jnp.concatenate(tiles, axis=0).astype(BF)
        qs_h.append(qs)

        s = jnp.where(vis_c, _dot_nt(kcb, qs), NEG)
        m = jnp.max(s, axis=0, keepdims=True)
        e = jnp.where(vis_c, jnp.exp2(s - m), 0.0)
        l = jnp.sum(e, axis=0, keepdims=True)
        p_c = e / jnp.where(l > 0.0, l, 1.0)
        oc_h.append(_dot(vct[h * HEAD_DIM:(h + 1) * HEAD_DIM], p_c.astype(BF)))
        imp = p_c[:, 0:tq]
        for g in range(1, NSA_G):
            imp = imp + p_c[:, g * tq:(g + 1) * tq]
        imp_h.append(imp)

    sel = _select_blocks(jnp.concatenate(imp_h, axis=1), forced, visible, TOP_K - 3)
    bias = jnp.where(sel, 0.0, NEG)
    if nb < LANES:
        bias = jnp.concatenate([bias, jnp.zeros((LANES - nb, NSA_KV * tq), F32)], axis=0)
    qa_h = []
    for h in range(NSA_KV):
        bias_t = bias[:, h * tq:(h + 1) * tq].T.astype(BF)
        qa_h.append(jnp.concatenate([qs_h[h], jnp.concatenate([bias_t] * NSA_G, axis=0)], axis=1))

    def put_scores(c, slot):
        ka = ks_ref[pl.ds(pl.multiple_of(c * tk, tk), tk), :]
        for h in range(NSA_KV):
            s_ref[slot, h] = _dot_nt(ka, qa_h[h])

    def softmax_pv(c, slot, stats, causal):
        new, probs = [], []
        for h in range(NSA_KV):
            m_i, l_i, _ = stats[h]
            s = s_ref[slot, h]
            if causal:
                kpos = c * tk + lax.broadcasted_iota(jnp.int32, (tk, wl), 0)
                s = jnp.where(kpos <= qpos, s, NEG)
            m_n = jnp.maximum(m_i, jnp.max(s, axis=0, keepdims=True))
            alpha = jnp.exp2(m_i - m_n)
            p = jnp.exp2(s - m_n)
            new.append((m_n, alpha * l_i + jnp.sum(p, axis=0, keepdims=True), alpha))
            probs.append(p.astype(BF))
        out = []
        for h in range(NSA_KV):
            m_n, l_n, alpha = new[h]
            vt = vst_ref[0, c, h * HEAD_DIM:(h + 1) * HEAD_DIM, :]
            out.append((m_n, l_n, alpha * stats[h][2] + _dot(vt, probs[h])))
        return tuple(out)

    def pair(p, stats):
        put_scores(2 * p + 1, 1)
        stats = softmax_pv(2 * p, 0, stats, False)
        put_scores(2 * p + 2, 0)
        return softmax_pv(2 * p + 1, 1, stats, False)

    init = (jnp.full((1, wl), NEG, F32), jnp.zeros((1, wl), F32), jnp.zeros((HEAD_DIM, wl), F32))
    n_full = q0 // tk
    n_pair = n_full // 2
    put_scores(0, 0)
    stats = lax.fori_loop(0, n_pair, pair, (init,) * NSA_KV)
    put_scores(jnp.minimum(2 * n_pair + 1, n_chunks - 1), 1)
    stats = softmax_pv(2 * n_pair, 0, stats, True)
    carry = lax.cond(n_full % 2 == 1, lambda st: softmax_pv(n_full, 1, st, True), lambda st: st, stats)

    sw_h = []
    for h in range(NSA_KV):
        s_parts = []
        for j in range(n_piece):
            start = q0 - WINDOW + j * LANES
            cl = pl.multiple_of(jnp.maximum(start, 0), LANES)
            sw = _dot_nt(kw_ref[pl.ds(cl, LANES), :], qs_h[h])
            if j == 0:
                sw = jnp.where(near_ok, sw, NEG)
            if j == n_piece - 1:
                sw = jnp.where(diag_ok, sw, NEG)
            else:
                sw = jnp.where(start >= 0, sw, NEG)
            s_parts.append(sw)
        sw_h.append(jnp.concatenate(s_parts, axis=0))

    for h in range(NSA_KV):
        m_s, l_s, acc_s = carry[h]
        o_s = acc_s / l_s
        sw = sw_h[h]
        mw = jnp.max(sw, axis=0, keepdims=True)
        pw = jnp.exp2(sw - mw)
        lw = jnp.sum(pw, axis=0, keepdims=True)
        pwb = pw.astype(BF)
        o_w = jnp.zeros((HEAD_DIM, wl), F32)
        for j in range(n_piece):
            cj = jnp.maximum(i - WINDOW // LANES + j, 0)
            vt = vwt_ref[0, cj, h * HEAD_DIM:(h + 1) * HEAD_DIM, :]
            o_w = o_w + _dot(vt, pwb[j * LANES:(j + 1) * LANES])
        o_w = o_w / lw

        def gate_row(j):
            return jnp.concatenate(
                [gate_t[(h * NSA_G + g) * 3 + j:(h * NSA_G + g) * 3 + j + 1, :] for g in range(NSA_G)], axis=1)

        o_t = gate_row(0) * oc_h[h] + gate_row(1) * o_s + gate_row(2) * o_w
        for gp in range(NSA_G // 2):
            pr = jnp.concatenate([o_t[:, (2 * gp) * tq:(2 * gp + 1) * tq],
                                  o_t[:, (2 * gp + 1) * tq:(2 * gp + 2) * tq]], axis=0)
            c0 = (h * (NSA_G // 2) + gp) * LANES
            out_ref[:, c0:c0 + LANES] = pr.T * _silu(nz_ref[:, c0:c0 + LANES])


def _nsa_prompt(q, misc, nz, cmp, ksa, vst, kwb, vwt, *, batch, seq):
    tq = NSA_TQ
    nq = seq // tq
    nb = seq // SEL_BLOCK
    assert tq == LANES and WINDOW % LANES == 0 and nb <= LANES
    row = lambda b, i: (b * nq + i, 0)
    return pl.pallas_call(
        functools.partial(_nsa_prompt_body, seq=seq),
        grid=(batch, nq),
        in_specs=[pl.BlockSpec((tq, NSA_WIDTH), row), pl.BlockSpec((tq, LANES), row),
                  pl.BlockSpec((tq, NSA_WIDTH), row),
                  pl.BlockSpec((1, nb, KVW), lambda b, i: (b, 0, 0)),
                  pl.BlockSpec((seq, HALF + LANES), lambda b, i: (b, 0)),
                  pl.BlockSpec((1, seq // PROJ_TM, HALF, PROJ_TM), lambda b, i: (b, 0, 0, 0)),
                  pl.BlockSpec((seq, HALF), lambda b, i: (b, 0)),
                  pl.BlockSpec((1, seq // LANES, HALF, LANES), lambda b, i: (b, 0, 0, 0))],
        out_specs=pl.BlockSpec((tq, NSA_WIDTH), row),
        out_shape=jax.ShapeDtypeStruct((batch * seq, NSA_WIDTH), F32),
        scratch_shapes=[pltpu.VMEM((2, NSA_KV, PROJ_TM, NSA_G * tq), F32)],
        compiler_params=_cparams(2),
        name="nsa_prompt",
    )(q, misc, nz, cmp.reshape(batch, nb, KVW), ksa, vst, kwb, vwt)


def _cmp_past_body(pt_ref, *refs):
    del pt_ref
    pages, wc_ref, out_ref = refs[:-2], refs[-2], refs[-1]
    rows = jnp.concatenate([pg[0, 0].T for pg in pages], axis=0)
    blocks = rows.reshape(rows.shape[0] // CMP_BLOCK, CMP_BLOCK, KVW)
    out_ref[0] = jnp.sum(blocks * wc_ref[...][None], axis=1)


def _page_spec(layer, page_size, r, pps):
    return pl.BlockSpec((1, 1, KVW, page_size), lambda b, c, pt: (layer, pt[b, c * pps + r], 0, 0))


def _rows_minor(cache):
    n_layers, n_pool, n_rows = cache.shape[:3]
    return jnp.transpose(cache, (0, 1, 3, 4, 5, 2)).reshape(n_layers, n_pool, KVW, n_rows)


def _cmp_past(cache4, page_table, wc, *, layer):
    page_size = cache4.shape[3]
    bd, n_pages = page_table.shape
    pps = PAGES_PER_STEP
    per_page = page_size // CMP_BLOCK
    grid_spec = pltpu.PrefetchScalarGridSpec(
        num_scalar_prefetch=1,
        grid=(bd, n_pages // pps),
        in_specs=[_page_spec(layer, page_size, r, pps) for r in range(pps)]
        + [pl.BlockSpec((CMP_BLOCK, KVW), lambda b, c, pt: (0, 0))],
        out_specs=pl.BlockSpec((1, pps * per_page, KVW), lambda b, c, pt: (b, c, 0)),
    )
    return pl.pallas_call(
        _cmp_past_body,
        grid_spec=grid_spec,
        out_shape=jax.ShapeDtypeStruct((bd, n_pages * per_page, KVW), F32),
        compiler_params=_cparams(2),
        name="cmp_past",
    )(page_table, *([cache4] * pps), wc)


def _nsa_sample_body(pt_ref, *refs, t_new, past_len, page_size):
    del pt_ref
    pps = PAGES_PER_STEP
    pages = refs[:pps]
    (q_ref, misc_ref, nz_ref, cmp_ref, kvs_ref, kvw_ref, win_ref, out_ref, newwin_ref,
     qs_ref, masked_ref, m_ref, l_ref, acc_ref, expand_ref) = refs[pps:]
    c = pl.program_id(1)
    n_chunks = pl.num_programs(1)
    rows = NSA_HEADS * t_new
    nbp = past_len // SEL_BLOCK
    keys = pps * page_size
    bps = keys // SEL_BLOCK
    t_of_row = lax.broadcasted_iota(jnp.int32, (rows, 1), 0) % t_new
    eye = jnp.where(lax.broadcasted_iota(jnp.int32, (rows, rows), 0)
                    == lax.broadcasted_iota(jnp.int32, (rows, rows), 1), 1.0, 0.0).astype(BF)

    @pl.when((pl.program_id(0) == 0) & (c == 0))
    def _():
        eb = lax.broadcasted_iota(jnp.int32, (2 * nbp, keys), 0) - (nbp - bps)
        ek = lax.broadcasted_iota(jnp.int32, (2 * nbp, keys), 1) // SEL_BLOCK
        expand_ref[...] = jnp.where(eb == ek, NEG, 0.0).astype(BF)

    @pl.when(c == 0)
    def _():
        qf = q_ref[...] * (HEAD_DIM ** -0.5)
        lane_head = lax.broadcasted_iota(jnp.int32, (t_new, LANES), 1) // HEAD_DIM
        tiles = []
        for hg in range(NSA_HEADS):
            h = hg // NSA_G
            t = qf[:, (hg // 2) * LANES:(hg // 2 + 1) * LANES]
            if hg % 2 != h:
                t = pltpu.roll(t, HEAD_DIM, axis=1)
            tiles.append(jnp.where(lane_head == h, t, 0.0))
        qs = jnp.concatenate(tiles, axis=0)
        qs_ref[...] = qs
        kc = cmp_ref[0]
        s = _dot_nt(kc[:, :HALF].astype(BF), qs.astype(BF))
        m = jnp.max(s, axis=0, keepdims=True)
        e = jnp.exp(s - m)
        p_c = e / jnp.sum(e, axis=0, keepdims=True)
        p_ct = _dot_nt(eye, p_c.astype(BF)).astype(BF)
        acc_ref[1] = _dot(p_ct, kc[:, HALF:].astype(BF))
        ri = lax.broadcasted_iota(jnp.int32, (rows, rows), 0)
        ci = lax.broadcasted_iota(jnp.int32, (rows, rows), 1)
        same = ((ri // (NSA_G * t_new)) == (ci // (NSA_G * t_new))) & ((ri % t_new) == (ci % t_new))
        gsum = jnp.where(same, 1.0, 0.0).astype(BF)
        p0, p1, p2 = _split3(p_c)
        imp = _dot(p0, gsum) + _dot(p1, gsum) + _dot(p2, gsum)
        blk = lax.broadcasted_iota(jnp.int32, (nbp, rows), 0)
        forced = (blk == 0) | (blk == nbp - 1)
        sel = _select_blocks(imp, forced, blk >= 0, TOP_K - 3)
        masked_ref[...] = _dot_nt(eye, jnp.where(sel, 0.0, 1.0).astype(BF)).astype(BF)
        m_ref[...] = jnp.full(m_ref.shape, NEG, F32)
        l_ref[...] = jnp.zeros(l_ref.shape, F32)
        acc_ref[0] = jnp.zeros(acc_ref.shape[1:], F32)

    qsb = qs_ref[...].astype(BF)
    start = pl.multiple_of((nbp - bps) - c * bps, bps)
    bias = _dot(masked_ref[...], expand_ref[pl.ds(start, nbp), :])
    kt = jnp.concatenate([pg[0, 0, :HALF, :].astype(BF) for pg in pages], axis=1)
    vt = jnp.concatenate([pg[0, 0, HALF:, :].astype(BF) for pg in pages], axis=1)
    s = _dot(qsb, kt) + bias
    m_i = m_ref[...]
    m_n = jnp.maximum(m_i, jnp.max(s, axis=1, keepdims=True))
    alpha = jnp.exp(m_i - m_n)
    p = jnp.exp(s - m_n)
    l_ref[...] = alpha * l_ref[...] + jnp.sum(p, axis=1, keepdims=True)
    acc_ref[0] = alpha * acc_ref[0] + _dot_nt(p.astype(BF), vt)
    m_ref[...] = m_n

    @pl.when(c == n_chunks - 1)
    def _():
        kvs = kvs_ref[...]
        tn = lax.broadcasted_iota(jnp.int32, (rows, t_new), 1)
        s_n = jnp.where(tn <= t_of_row, _dot_nt(qsb, kvs[:, :HALF].astype(BF)), NEG)
        m_i = m_ref[...]
        m_n = jnp.maximum(m_i, jnp.max(s_n, axis=1, keepdims=True))
        alpha = jnp.exp(m_i - m_n)
        p_n = jnp.exp(s_n - m_n)
        l_s = alpha * l_ref[...] + jnp.sum(p_n, axis=1, keepdims=True)
        o_s = (alpha * acc_ref[0] + _dot(p_n.astype(BF), kvs[:, HALF:].astype(BF))) / l_s
        wbt = win_ref[0, 0]
        w_buf = wbt.shape[1]
        kvw = kvw_ref[...]
        rb = lax.broadcasted_iota(jnp.int32, (rows, w_buf), 1)
        s_b = jnp.where(w_buf + t_of_row - rb < WINDOW, _dot(qsb, wbt[:HALF].astype(BF)), NEG)
        s_w = jnp.where(tn <= t_of_row, _dot_nt(qsb, kvw[:, :HALF].astype(BF)), NEG)
        m_w = jnp.maximum(jnp.max(s_b, axis=1, keepdims=True), jnp.max(s_w, axis=1, keepdims=True))
        p_b = jnp.exp(s_b - m_w)
        p_w = jnp.exp(s_w - m_w)
        l_w = jnp.sum(p_b, axis=1, keepdims=True) + jnp.sum(p_w, axis=1, keepdims=True)
        o_w = (_dot_nt(p_b.astype(BF), wbt[HALF:].astype(BF)) + _dot(p_w.astype(BF), kvw[:, HALF:].astype(BF))) / l_w
        shifted = pltpu.roll(wbt, w_buf - t_new, axis=1)
        newwin_ref[0, :, :w_buf - LANES] = shifted[:, :w_buf - LANES]
        tail = jnp.concatenate([jnp.zeros((LANES - t_new, KVW), F32), kvw], axis=0).T
        lane = lax.broadcasted_iota(jnp.int32, (KVW, LANES), 1)
        newwin_ref[0, :, w_buf - LANES:] = jnp.where(lane >= LANES - t_new, tail, shifted[:, w_buf - LANES:])
        gates = _sigmoid(misc_ref[...])
        o_c = acc_ref[1]
        nz = nz_ref[...]
        for hg in range(NSA_HEADS):
            h = hg // NSA_G
            r0 = hg * t_new
            o = (gates[:, hg * 3:hg * 3 + 1] * o_c[r0:r0 + t_new]
                 + gates[:, hg * 3 + 1:hg * 3 + 2] * o_s[r0:r0 + t_new]
                 + gates[:, hg * 3 + 2:hg * 3 + 3] * o_w[r0:r0 + t_new])
            c0 = hg * HEAD_DIM
            out_ref[:, c0:c0 + HEAD_DIM] = o[:, h * HEAD_DIM:(h + 1) * HEAD_DIM] * _silu(nz[:, c0:c0 + HEAD_DIM])


def _nsa_sample(q, misc, nz, cmp_past, kvs, kvw, cache4, win4, page_table, *, layer, t_new):
    page_size = cache4.shape[3]
    bd, n_pages = page_table.shape
    past_len = n_pages * page_size
    w_buf = win4.shape[3]
    pps = PAGES_PER_STEP
    nbp = past_len // SEL_BLOCK
    rows = NSA_HEADS * t_new
    row = lambda b, c, pt: (b, 0)
    grid_spec = pltpu.PrefetchScalarGridSpec(
        num_scalar_prefetch=1,
        grid=(bd, n_pages // pps),
        in_specs=[_page_spec(layer, page_size, r, pps) for r in range(pps)]
        + [pl.BlockSpec((t_new, NSA_WIDTH), row), pl.BlockSpec((t_new, LANES), row),
           pl.BlockSpec((t_new, NSA_WIDTH), row),
           pl.BlockSpec((1, nbp, KVW), lambda b, c, pt: (b, 0, 0)),
           pl.BlockSpec((t_new, KVW), row), pl.BlockSpec((t_new, KVW), row),
           pl.BlockSpec((1, 1, KVW, w_buf), lambda b, c, pt: (layer, b, 0, 0))],
        out_specs=[pl.BlockSpec((t_new, NSA_WIDTH), row),
                   pl.BlockSpec((1, KVW, w_buf), lambda b, c, pt: (b, 0, 0))],
        scratch_shapes=[pltpu.VMEM((rows, LANES), F32), pltpu.VMEM((rows, nbp), BF),
                        pltpu.VMEM((rows, 1), F32), pltpu.VMEM((rows, 1), F32),
                        pltpu.VMEM((2, rows, LANES), F32), pltpu.VMEM((2 * nbp, pps * page_size), BF)],
    )
    return pl.pallas_call(
        functools.partial(_nsa_sample_body, t_new=t_new, past_len=past_len, page_size=page_size),
        grid_spec=grid_spec,
        out_shape=[jax.ShapeDtypeStruct((bd * t_new, NSA_WIDTH), F32),
                   jax.ShapeDtypeStruct((bd, KVW, w_buf), F32)],
        compiler_params=_cparams(2),
        name="nsa_sample",
    )(page_table, *([cache4] * pps), q, misc, nz, cmp_past, kvs, kvw, win4)


def _gla_body(gla_ref, misc_ref, a2_ref, ab_ref, gn_ref, s0_ref, out_ref, sout_ref, st_ref, *, chunk, mxu_dtype):
    j = pl.program_id(1)
    n_steps = pl.num_programs(1)
    tb = gla_ref.shape[0]
    sub = min(GLA_SUB, chunk)
    hk = GLA_HEADS * chunk

    @pl.when(j == 0)
    def _():
        s0 = s0_ref[0].reshape(GLA_KW, GLA_DV)
        ri = lax.broadcasted_iota(jnp.int32, (GLA_WIDTH, GLA_DV), 0) % GLA_DV
        ci = lax.broadcasted_iota(jnp.int32, (GLA_WIDTH, GLA_DV), 1)
        pick = jnp.where(ri == ci, 1.0, 0.0).astype(BF)
        wide = sum(_dot_nt(pick, part) for part in _split3(s0))
        st_ref[...] = jnp.where(_gla_diag(GLA_WIDTH, GLA_DV, GLA_KW, GLA_DK), wide, 0.0)

    diag_sv = _gla_diag(GLA_WIDTH, GLA_DV, GLA_KW, GLA_DK)
    diag_k = _gla_diag(hk, chunk, GLA_KW, GLA_DK)
    diag_v = _gla_diag(hk, chunk, GLA_WIDTH, GLA_DV)
    tril = jnp.where(lax.broadcasted_iota(jnp.int32, (chunk, chunk), 0)
                     >= lax.broadcasted_iota(jnp.int32, (chunk, chunk), 1), 1.0, 0.0).astype(BF)
    srow = lax.broadcasted_iota(jnp.int32, (chunk, GLA_KW), 0)
    ri = lax.broadcasted_iota(jnp.int32, (GLA_WIDTH, GLA_WIDTH), 0) // GLA_DV
    ci = lax.broadcasted_iota(jnp.int32, (GLA_WIDTH, GLA_WIDTH), 1) // GLA_DV
    head_mean = jnp.where(ri == ci, 1.0 / GLA_DV, 0.0).astype(BF)
    a_col = lax.broadcasted_iota(jnp.int32, (sub, hk), 1) % chunk
    a_row = lax.broadcasted_iota(jnp.int32, (sub, hk), 0)

    def cast(a):
        return a.astype(mxu_dtype)

    def one_chunk(ci_, carry):
        r0 = pl.multiple_of(ci_ * chunk, chunk)
        blk = gla_ref[pl.ds(r0, chunk), :]
        gq = blk[:, 0:GLA_KW] * (GLA_DK ** -0.5)
        gk = blk[:, GLA_KW:2 * GLA_KW]
        gv = blk[:, 2 * GLA_KW:2 * GLA_KW + GLA_WIDTH]
        gz = blk[:, 2 * GLA_KW + GLA_WIDTH:]
        a_logit = _dot(cast(misc_ref[pl.ds(r0, chunk), :]), cast(a2_ref[...])) + ab_ref[...]
        log_a = (jnp.minimum(a_logit, 0.0) - jnp.log(1.0 + jnp.exp(-jnp.abs(a_logit)))) * (1.0 / GLA_TAU)
        la0, la1, la2 = _split3(log_a)
        bcum = _dot(tril, la0) + _dot(tril, la1) + _dot(tril, la2)
        st = st_ref[...]
        inter = _dot_nt(cast(gq * jnp.exp(bcum)), cast(st))
        v_bd = jnp.where(diag_v, jnp.concatenate([gv] * GLA_HEADS, axis=0), 0.0)
        v_bd = cast(v_bd)
        parts = []
        for sb in range(chunk // sub):
            t0 = sb * sub
            ref_row = bcum[t0:t0 + 1, :]
            qd = gq[t0:t0 + sub] * jnp.exp(bcum[t0:t0 + sub] - ref_row)
            kd = gk * jnp.exp(jnp.where(srow < t0 + sub, ref_row - bcum, NEG))
            k_bd = jnp.where(diag_k, jnp.concatenate([kd] * GLA_HEADS, axis=0), 0.0)
            att = _dot_nt(cast(qd), cast(k_bd))
            att = jnp.where(a_col <= a_row + t0, att, 0.0)
            parts.append(_dot(cast(att), v_bd))
        o = inter + jnp.concatenate(parts, axis=0)
        b_last = bcum[chunk - 1:chunk, :]
        kdec = gk * jnp.exp(b_last - bcum)
        upd = _dot(cast(_transpose_mxu(gv, mxu_dtype)), cast(kdec))
        st_ref[...] = st * jnp.exp(b_last) + jnp.where(diag_sv, upd, 0.0)
        o2 = o * o
        q0, q1, _ = _split3(o2)
        ms = _dot(q0, head_mean) + _dot(q1, head_mean)
        out_ref[pl.ds(r0, chunk), :] = o * lax.rsqrt(ms + EPS) * gn_ref[...] * _silu(gz)
        return carry

    lax.fori_loop(0, tb // chunk, one_chunk, 0, unroll=min(4, tb // chunk))

    @pl.when(j == n_steps - 1)
    def _():
        st = st_ref[...]
        acc = st[0:GLA_DV]
        for h in range(1, GLA_HEADS):
            acc = acc + st[h * GLA_DV:(h + 1) * GLA_DV]
        sout_ref[0] = _transpose_mxu(acc, F32).reshape(GLA_HEADS, GLA_DK, GLA_DV)


def _gla_diag(n_rows, row_group, n_cols, col_group):
    r = lax.broadcasted_iota(jnp.int32, (n_rows, n_cols), 0) // row_group
    c = lax.broadcasted_iota(jnp.int32, (n_rows, n_cols), 1) // col_group
    return r == c


def _gla(gla, misc, a2p, ab, gn, s0, *, batch, seq):
    chunk = min(GLA_CHUNK, seq)
    tb = min(PROJ_TM, seq)
    steps = seq // tb
    row = lambda b, j: (b * steps + j, 0)
    fixed = lambda b, j: (0, 0)
    return pl.pallas_call(
        functools.partial(_gla_body, chunk=chunk, mxu_dtype=BF if chunk >= 16 else F32),
        grid=(batch, steps),
        in_specs=[pl.BlockSpec((tb, W_GLA), row), pl.BlockSpec((tb, LANES), row),
                  pl.BlockSpec((LANES, GLA_KW), fixed), pl.BlockSpec((1, GLA_KW), fixed),
                  pl.BlockSpec((1, GLA_WIDTH), fixed),
                  pl.BlockSpec((1, GLA_HEADS, GLA_DK, GLA_DV), lambda b, j: (b, 0, 0, 0))],
        out_specs=[pl.BlockSpec((tb, GLA_WIDTH), row),
                   pl.BlockSpec((1, GLA_HEADS, GLA_DK, GLA_DV), lambda b, j: (b, 0, 0, 0))],
        out_shape=[jax.ShapeDtypeStruct((batch * seq, GLA_WIDTH), F32),
                   jax.ShapeDtypeStruct((batch, GLA_HEADS, GLA_DK, GLA_DV), F32)],
        scratch_shapes=[pltpu.VMEM((GLA_WIDTH, GLA_KW), F32)],
        compiler_params=_cparams(2),
        name="gla_prompt" if seq > GLA_CHUNK else "gla_sample",
    )(gla, misc, a2p, ab, gn, s0)


def _out_body(x_ref, nsa_ref, gla_ref, conv_ref, halo_ref, buf_ref, cw_ref, wo_ref, g_ref, y_ref, nbuf_ref,
              *, seq, tm):
    cv = conv_ref[...]
    cb = cv[:, 0:CONV_WIDTH]
    u = cv[:, CONV_WIDTH:2 * CONV_WIDTH] * cv[:, 2 * CONV_WIDTH:3 * CONV_WIDTH]
    cz = cv[:, 3 * CONV_WIDTH:]
    rows = lax.broadcasted_iota(jnp.int32, (tm, CONV_WIDTH), 0)
    if seq >= tm:
        first = pl.program_id(0) % (seq // tm) == 0
        hv = halo_ref[...]
        hu = hv[:, CONV_WIDTH:2 * CONV_WIDTH] * hv[:, 2 * CONV_WIDTH:3 * CONV_WIDTH]
        bufv = buf_ref[0]
        p1 = jnp.where(first, bufv[1:2], hu[7:8])
        p2 = jnp.where(first, bufv[0:1], hu[6:7])
        prev1 = jnp.where(rows == 0, p1, pltpu.roll(u, 1, axis=0))
        prev2 = jnp.where(rows == 0, p2, jnp.where(rows == 1, p1, pltpu.roll(u, 2, axis=0)))
        nbuf_ref[0] = u[tm - (CONV_K - 1):, :]
    else:
        nseq = tm // seq
        bufv = buf_ref[...]
        b0 = jnp.broadcast_to(bufv[:, 0:1, :], (nseq, seq, CONV_WIDTH)).reshape(tm, CONV_WIDTH)
        b1 = jnp.broadcast_to(bufv[:, 1:2, :], (nseq, seq, CONV_WIDTH)).reshape(tm, CONV_WIDTH)
        t = rows % seq
        prev1 = jnp.where(t == 0, b1, pltpu.roll(u, 1, axis=0))
        prev2 = jnp.where(t == 0, b0, jnp.where(t == 1, b1, pltpu.roll(u, 2, axis=0)))
        nbuf_ref[...] = u.reshape(nseq, seq, CONV_WIDTH)[:, seq - (CONV_K - 1):, :]
    cw = cw_ref[...]
    y = prev2 * cw[0:1] + prev1 * cw[1:2] + u * cw[2:3]
    o_conv = cb * y * _silu(cz)
    acc = _dot(nsa_ref[...].astype(BF), wo_ref[0:NSA_WIDTH, :])
    acc = acc + _dot(gla_ref[...].astype(BF), wo_ref[NSA_WIDTH:NSA_WIDTH + GLA_WIDTH, :])
    acc = acc + _dot(o_conv.astype(BF), wo_ref[NSA_WIDTH + GLA_WIDTH:, :])
    normed = acc * lax.rsqrt(jnp.mean(acc * acc, axis=-1, keepdims=True) + EPS) * g_ref[...]
    y_ref[...] = x_ref[...] + normed


def _out(x2, o_nsa, o_gla, conv, buf, cw, wo, g, *, batch, seq):
    n, d = x2.shape
    tm = PROJ_TM if seq >= PROJ_TM else n
    steps = n // tm
    row = lambda i: (i, 0)
    fixed = lambda i: (0, 0)
    if seq >= tm:
        per_seq = seq // tm
        halo_spec = pl.BlockSpec((8, W_CONV), lambda i: (jnp.maximum(i * (tm // 8) - 1, 0), 0))
        buf_spec = pl.BlockSpec((1, CONV_K - 1, CONV_WIDTH), lambda i: (i // per_seq, 0, 0))
    else:
        halo_spec = pl.BlockSpec((8, W_CONV), fixed)
        buf_spec = pl.BlockSpec((batch, CONV_K - 1, CONV_WIDTH), lambda i: (0, 0, 0))
    return pl.pallas_call(
        functools.partial(_out_body, seq=seq, tm=tm),
        grid=(steps,),
        in_specs=[pl.BlockSpec((tm, d), row), pl.BlockSpec((tm, NSA_WIDTH), row),
                  pl.BlockSpec((tm, GLA_WIDTH), row), pl.BlockSpec((tm, W_CONV), row),
                  halo_spec, buf_spec,
                  pl.BlockSpec((CONV_K, CONV_WIDTH), fixed), pl.BlockSpec((d, d), fixed),
                  pl.BlockSpec((1, d), fixed)],
        out_specs=[pl.BlockSpec((tm, d), row), buf_spec],
        out_shape=[jax.ShapeDtypeStruct((n, d), F32),
                   jax.ShapeDtypeStruct((batch, CONV_K - 1, CONV_WIDTH), F32)],
        compiler_params=_cparams(1),
        name="out_prompt" if seq >= PROJ_TM else "out_sample",
    )(x2, o_nsa, o_gla, conv, conv, buf, cw, wo, g)


def _reorder_w_in(w):
    o_gate = NSA_WIDTH + 3 * KVW
    o_nz = o_gate + N_GATE
    o_gla = o_nz + NSA_WIDTH
    o_a = o_gla + 2 * GLA_KW + GLA_WIDTH
    o_gz = o_a + GLA_RANK
    o_conv = o_gz + GLA_WIDTH
    pad = jnp.zeros((w.shape[0], LANES - N_GATE - GLA_RANK), w.dtype)
    return jnp.concatenate([w[:, :o_gate], w[:, o_nz:o_gla], w[:, o_gla:o_a], w[:, o_gz:o_conv], w[:, o_conv:],
                            w[:, o_gate:o_nz], w[:, o_a:o_gz], pad], axis=1)


def _layer(x, l, P, *, prompt, cache_cmp_kv=None, cache_slc_kv=None, cache_win_kv=None, page_table=None,
           gla_s0=None, conv_buf=None):
    batch, seq, d = x.shape
    x2 = x.reshape(batch * seq, d)
    outs = _proj(x2, P['norm_pre'][l], P['w_in'][l], P['w_cmp'][l], batch=batch, seq=seq, prompt=prompt)
    q, kvc, kvs, kvw, nz, gla, conv, misc = outs[:8]
    kv_shape = (batch, seq, 2, NSA_KV, HEAD_DIM)
    if prompt:
        ksb, kwb, vst, vwt, cmp = outs[8:]
        o_nsa = _nsa_prompt(q, misc, nz, cmp, ksb, vst, kwb, vwt, batch=batch, seq=seq)
        w_keep = min(WINDOW, seq)
        new_w = kvw.reshape(kv_shape)[:, seq - w_keep:]
    else:
        cmp_past = _cmp_past(cache_cmp_kv, page_table, P['w_cmp'][l], layer=l)
        o_nsa, new_w = _nsa_sample(q, misc, nz, cmp_past, kvs, kvw, cache_slc_kv, cache_win_kv, page_table,
                                   layer=l, t_new=seq)
        w_buf = new_w.shape[2]
        new_w = jnp.transpose(new_w.reshape((batch,) + kv_shape[2:] + (w_buf,)), (0, 4, 1, 2, 3))
    o_gla, s_gla = _gla(gla, misc, P['gla_a2'][l], P['gla_ab'][l], P['gla_norm'][l], gla_s0, batch=batch, seq=seq)
    y2, new_buf = _out(x2, o_nsa, o_gla, conv, conv_buf, P['conv_w'][l], P['w_out'][l], P['norm_post'][l],
                       batch=batch, seq=seq)
    return y2.reshape(batch, seq, d), (kvc.reshape(kv_shape), kvs.reshape(kv_shape), new_w, s_gla, new_buf)


def kernel(x_prompt, x_sample, cache_cmp_kv, cache_slc_kv, cache_win_kv, state_gla, state_conv, page_table,
           norm_pre, norm_post, w_in, w_out, w_cmp_k, w_cmp_v, gla_a2, gla_ab, gla_norm, conv_w):
    depth = w_in.shape[0]
    d = w_in.shape[1]
    ones = jnp.ones((1, HALF), F32)
    a2p = jnp.zeros((depth, LANES, GLA_KW), F32).at[:, N_GATE:N_GATE + GLA_RANK, :].set(gla_a2)
    P = {
        'norm_pre': norm_pre.reshape(depth, 1, d),
        'norm_post': norm_post.reshape(depth, 1, d),
        'w_in': jax.vmap(_reorder_w_in)(w_in).astype(BF),
        'w_out': w_out.astype(BF),
        'w_cmp': jnp.concatenate([w_cmp_k[:, :, None] * ones, w_cmp_v[:, :, None] * ones], axis=2),
        'gla_a2': a2p,
        'gla_ab': gla_ab.reshape(depth, 1, GLA_KW),
        'gla_norm': jnp.tile(gla_norm, (1, GLA_HEADS)).reshape(depth, 1, GLA_WIDTH),
        'conv_w': conv_w,
    }
    xp, xs = x_prompt, x_sample
    bp = xp.shape[0]
    st_p, st_s = [], []
    cache_cmp_kv, cache_slc_kv, cache_win_kv = (_rows_minor(c) for c in (cache_cmp_kv, cache_slc_kv, cache_win_kv))
    for l in range(depth):
        xp, sp = _layer(xp, l, P, prompt=True,
                        gla_s0=jnp.zeros((bp, GLA_HEADS, GLA_DK, GLA_DV), F32),
                        conv_buf=jnp.zeros((bp, CONV_K - 1, CONV_WIDTH), F32))
        xs, ss = _layer(xs, l, P, prompt=False, cache_cmp_kv=cache_cmp_kv, cache_slc_kv=cache_slc_kv,
                        cache_win_kv=cache_win_kv, page_table=page_table,
                        gla_s0=state_gla[l], conv_buf=state_conv[l])
        st_p.append(sp)
        st_s.append(ss)
    stack = lambda sts, k: jnp.stack([s[k] for s in sts])
    return (xp, xs, stack(st_p, 0), stack(st_s, 0), stack(st_p, 1), stack(st_s, 1), stack(st_p, 2), stack(st_s, 2),
            stack(st_p, 3), stack(st_s, 3), stack(st_p, 4), stack(st_s, 4))
```

```python
import functools

import jax
import jax.numpy as jnp
from jax import lax
from jax.experimental import pallas as pl
from jax.experimental.pallas import tpu as pltpu

F32 = jnp.float32
BF = jnp.bfloat16

HEAD_DIM = 64
NSA_KV = 2
NSA_G = 4
NSA_HEADS = NSA_KV * NSA_G
NSA_WIDTH = NSA_HEADS * HEAD_DIM
KVW = 2 * NSA_KV * HEAD_DIM
HALF = NSA_KV * HEAD_DIM
CMP_BLOCK = 64
SEL_BLOCK = 64
TOP_K = 16
WINDOW = 512
GLA_HEADS = 4
GLA_DK = 32
GLA_DV = 64
GLA_KW = GLA_HEADS * GLA_DK
GLA_WIDTH = GLA_HEADS * GLA_DV
GLA_RANK = 16
GLA_TAU = 16.0
GLA_CHUNK = 64
GLA_SUB = 16
GLA_TILE = 256
GLA_PROBLEMS = 8
CONV_WIDTH = 256
CONV_K = 3
N_GATE = 3 * NSA_HEADS
EPS = 1e-6
NEG = -1e30
LOG2E = 1.4426950408889634

LANES = 128
VMEM_LIMIT = 48 * 1024 * 1024

C_Q = 0
C_CMP = C_Q + NSA_WIDTH
C_SLC = C_CMP + KVW
C_WIN = C_SLC + KVW
C_NZ = C_WIN + KVW
C_GLA = C_NZ + NSA_WIDTH
W_GLA = 2 * GLA_KW + 2 * GLA_WIDTH
C_CONV = C_GLA + W_GLA
W_CONV = 4 * CONV_WIDTH
C_MISC = C_CONV + W_CONV
PROJ_PAD = C_MISC + LANES

PROJ_TM = 512
NSA_TQ = 128
PAGES_PER_STEP = 32


def _cparams(n_axes, vmem=VMEM_LIMIT):
    return pltpu.CompilerParams(dimension_semantics=("arbitrary",) * n_axes, vmem_limit_bytes=vmem)


def _dot(a, b):
    return jnp.dot(a, b, preferred_element_type=F32)


def _dot_nt(a, b):
    return lax.dot_general(a, b, (((1,), (1,)), ((), ())), preferred_element_type=F32)


def _split3(a):
    a0 = a.astype(BF)
    r = a - a0.astype(F32)
    a1 = r.astype(BF)
    a2 = (r - a1.astype(F32)).astype(BF)
    return a0, a1, a2


def _transpose_mxu(y, dtype):
    n = y.shape[1]
    eye = jnp.where(lax.broadcasted_iota(jnp.int32, (n, n), 0) == lax.broadcasted_iota(jnp.int32, (n, n), 1),
                    1.0, 0.0).astype(BF)
    if dtype == BF:
        return _dot_nt(eye, y.astype(BF))
    return sum(_dot_nt(eye, part) for part in _split3(y))


def _silu(x):
    return x * (1.0 / (1.0 + jnp.exp(-x)))


def _sigmoid(x):
    return 1.0 / (1.0 + jnp.exp(-x))


def _proj_body(x_ref, g_ref, w_ref, wc_ref, *outs, prompt, seq):
    x = x_ref[...]
    h = x * lax.rsqrt(jnp.mean(x * x, axis=-1, keepdims=True) + EPS) * g_ref[...]
    hb = h.astype(BF)

    def proj(c0, n):
        return _dot(hb, w_ref[:, c0:c0 + n])

    q_ref, kvc_ref, kvs_ref, kvw_ref, nz_ref, gla_ref, conv_ref, misc_ref = outs[:8]
    q_ref[...] = proj(C_Q, NSA_WIDTH)
    kvc = proj(C_CMP, KVW)
    kvs = proj(C_SLC, KVW)
    kvw = proj(C_WIN, KVW)
    kvc_ref[...] = kvc
    kvs_ref[...] = kvs
    kvw_ref[...] = kvw
    nz_ref[...] = proj(C_NZ, NSA_WIDTH)
    gla_ref[...] = proj(C_GLA, W_GLA)
    conv_ref[...] = proj(C_CONV, W_CONV)
    misc_ref[...] = proj(C_MISC, LANES)
    if prompt:
        ksb_ref, kwb_ref, vst_ref, vwt_ref, cmp_ref = outs[8:]
        tm = x.shape[0]
        blk = (pl.program_id(0) % (seq // tm)) * (tm // SEL_BLOCK) \
            + lax.broadcasted_iota(jnp.int32, (tm, LANES), 0) // SEL_BLOCK
        onehot = jnp.where(lax.broadcasted_iota(jnp.int32, (tm, LANES), 1) == blk, 1.0, 0.0)
        ksb_ref[...] = jnp.concatenate([kvs[:, :HALF], onehot], axis=1).astype(BF)
        kwb_ref[...] = kvw[:, :HALF].astype(BF)
        vst_ref[0, 0] = kvs[:, HALF:].T.astype(BF)
        for j in range(tm // LANES):
            vwt_ref[0, j] = kvw[j * LANES:(j + 1) * LANES, HALF:].T.astype(BF)
        cmp_ref[...] = jnp.sum(kvc.reshape(tm // CMP_BLOCK, CMP_BLOCK, KVW) * wc_ref[...][None], axis=1)


def _proj(x2, g, w, wc, *, batch, seq, prompt):
    n, d = x2.shape
    tm = PROJ_TM if prompt else n
    steps = n // tm
    per_seq = seq // tm if prompt else 1
    row = lambda i: (i, 0)
    fixed = lambda i: (0, 0)
    widths = [NSA_WIDTH, KVW, KVW, KVW, NSA_WIDTH, W_GLA, W_CONV, LANES]
    out_shape = [jax.ShapeDtypeStruct((n, c), F32) for c in widths]
    out_specs = [pl.BlockSpec((tm, c), row) for c in widths]
    if prompt:
        out_shape += [jax.ShapeDtypeStruct((n, HALF + LANES), BF), jax.ShapeDtypeStruct((n, HALF), BF),
                      jax.ShapeDtypeStruct((batch, seq // tm, HALF, tm), BF),
                      jax.ShapeDtypeStruct((batch, seq // LANES, HALF, LANES), BF),
                      jax.ShapeDtypeStruct((n // CMP_BLOCK, KVW), F32)]
        out_specs += [pl.BlockSpec((tm, HALF + LANES), row), pl.BlockSpec((tm, HALF), row),
                      pl.BlockSpec((1, 1, HALF, tm), lambda i: (i // per_seq, i % per_seq, 0, 0)),
                      pl.BlockSpec((1, tm // LANES, HALF, LANES), lambda i: (i // per_seq, i % per_seq, 0, 0)),
                      pl.BlockSpec((tm // CMP_BLOCK, KVW), row)]
    return pl.pallas_call(
        functools.partial(_proj_body, prompt=prompt, seq=seq),
        grid=(steps,),
        in_specs=[pl.BlockSpec((tm, d), row), pl.BlockSpec((1, d), fixed),
                  pl.BlockSpec((d, PROJ_PAD), fixed), pl.BlockSpec((CMP_BLOCK, KVW), fixed)],
        out_specs=out_specs,
        out_shape=out_shape,
        compiler_params=_cparams(1),
        name="proj_prompt" if prompt else "proj_sample",
    )(x2, g, w, wc)


def _select_blocks(imp, forced, visible, n_pick):
    nb = imp.shape[0]
    blk = lax.broadcasted_iota(jnp.int32, imp.shape, 0)
    score = jnp.where(forced, -jnp.inf, jnp.where(visible, imp, -1.0))
    sel = forced
    for _ in range(n_pick):
        m = jnp.max(score, axis=0, keepdims=True)
        first = jnp.min(jnp.where(score == m, blk, nb), axis=0, keepdims=True)
        hit = blk == first
        sel = sel | hit
        score = jnp.where(hit, -jnp.inf, score)
    return sel


def _nsa_prompt_body(q_ref, misc_ref, nz_ref, cmp_ref, ks_ref, vst_ref, kw_ref, vwt_ref, out_ref, s_ref, *, seq):
    tq = NSA_TQ
    tk = PROJ_TM
    nb = seq // SEL_BLOCK
    n_chunks = seq // tk
    wl = NSA_G * tq
    n_piece = WINDOW // LANES + 1
    i = pl.program_id(1)
    q0 = i * tq
    qf = q_ref[...] * (HEAD_DIM ** -0.5 * LOG2E)
    gate_t = _sigmoid(misc_ref[...]).T
    lane_head = lax.broadcasted_iota(jnp.int32, (tq, LANES), 1) // HEAD_DIM
    kc = cmp_ref[0]
    kcb = kc[:, :HALF].astype(BF)
    vct = kc[:, HALF:].T.astype(BF)

    ql = lax.broadcasted_iota(jnp.int32, (1, wl), 1) % tq
    qpos = q0 + ql
    blk_w = lax.broadcasted_iota(jnp.int32, (nb, wl), 0)
    vis_c = blk_w * CMP_BLOCK + (CMP_BLOCK - 1) <= qpos
    blk2 = lax.broadcasted_iota(jnp.int32, (nb, NSA_KV * tq), 0)
    cur = (q0 + lax.broadcasted_iota(jnp.int32, (nb, NSA_KV * tq), 1) % tq) // SEL_BLOCK
    forced = (blk2 == 0) | (blk2 == cur) | (blk2 == cur - 1)
    visible = blk2 <= cur
    row_p = lax.broadcasted_iota(jnp.int32, (LANES, wl), 0)
    near_ok = row_p > ql
    diag_ok = row_p <= ql

    qs_h, oc_h, imp_h = [], [], []
    for h in range(NSA_KV):
        tiles = []
        for g in range(NSA_G):
            c = h * NSA_G + g
            t = qf[:, (c // 2) * LANES:(c // 2 + 1) * LANES]
            if c % 2 != h:
                t = pltpu.roll(t, HEAD_DIM, axis=1)
            tiles.append(jnp.where(lane_head == h, t, 0.0))
        qs = jnp.concatenate(tiles, axis=0).astype(BF)
        qs_h.append(qs)

        s = jnp.where(vis_c, _dot_nt(kcb, qs), NEG)
        m = jnp.max(s, axis=0, keepdims=True)
        e = jnp.where(vis_c, jnp.exp2(s - m), 0.0)
        l = jnp.sum(e, axis=0, keepdims=True)
        p_c = e / jnp.where(l > 0.0, l, 1.0)
        oc_h.append(_dot(vct[h * HEAD_DIM:(h + 1) * HEAD_DIM], p_c.astype(BF)))
        imp = p_c[:, 0:tq]
        for g in range(1, NSA_G):
            imp = imp + p_c[:, g * tq:(g + 1) * tq]
        imp_h.append(imp)

    sel = _select_blocks(jnp.concatenate(imp_h, axis=1), forced, visible, TOP_K - 3)
    bias = jnp.where(sel, 0.0, NEG)
    if nb < LANES:
        bias = jnp.concatenate([bias, jnp.zeros((LANES - nb, NSA_KV * tq), F32)], axis=0)
    qa_h = []
    for h in range(NSA_KV):
        bias_t = bias[:, h * tq:(h + 1) * tq].T.astype(BF)
        qa_h.append(jnp.concatenate([qs_h[h], jnp.concatenate([bias_t] * NSA_G, axis=0)], axis=1))

    sw_h = []
    for h in range(NSA_KV):
        s_parts = []
        for j in range(n_piece):
            start = q0 - WINDOW + j * LANES
            cl = pl.multiple_of(jnp.maximum(start, 0), LANES)
            sw = _dot_nt(kw_ref[pl.ds(cl, LANES), :], qs_h[h])
            if j == 0:
                sw = jnp.where(near_ok, sw, NEG)
            if j == n_piece - 1:
                sw = jnp.where(diag_ok, sw, NEG)
            else:
                sw = jnp.where(start >= 0, sw, NEG)
            s_parts.append(sw)
        sw_h.append(jnp.concatenate(s_parts, axis=0))

    def put_scores(c, slot):
        ka = ks_ref[pl.ds(pl.multiple_of(c * tk, tk), tk), :]
        for h in range(NSA_KV):
            s_ref[slot, h] = _dot_nt(ka, qa_h[h])

    def softmax_pv(c, slot, stats, causal):
        new, probs = [], []
        for h in range(NSA_KV):
            m_i, l_i, _ = stats[h]
            s = s_ref[slot, h]
            if causal:
                kpos = c * tk + lax.broadcasted_iota(jnp.int32, (tk, wl), 0)
                s = jnp.where(kpos <= qpos, s, NEG)
            m_n = jnp.maximum(m_i, jnp.max(s, axis=0, keepdims=True))
            alpha = jnp.exp2(m_i - m_n)
            p = jnp.exp2(s - m_n)
            new.append((m_n, alpha * l_i + jnp.sum(p, axis=0, keepdims=True), alpha))
            probs.append(p.astype(BF))
        out = []
        for h in range(NSA_KV):
            m_n, l_n, alpha = new[h]
            vt = vst_ref[0, c, h * HEAD_DIM:(h + 1) * HEAD_DIM, :]
            out.append((m_n, l_n, alpha * stats[h][2] + _dot(vt, probs[h])))
        return tuple(out)

    def pair(p, stats):
        put_scores(2 * p + 1, 1)
        stats = softmax_pv(2 * p, 0, stats, False)
        put_scores(2 * p + 2, 0)
        return softmax_pv(2 * p + 1, 1, stats, False)

    init = (jnp.full((1, wl), NEG, F32), jnp.zeros((1, wl), F32), jnp.zeros((HEAD_DIM, wl), F32))
    n_full = q0 // tk
    n_pair = n_full // 2
    put_scores(0, 0)
    stats = lax.fori_loop(0, n_pair, pair, (init,) * NSA_KV)
    put_scores(jnp.minimum(2 * n_pair + 1, n_chunks - 1), 1)
    stats = softmax_pv(2 * n_pair, 0, stats, True)
    carry = lax.cond(n_full % 2 == 1, lambda st: softmax_pv(n_full, 1, st, True), lambda st: st, stats)

    for h in range(NSA_KV):
        m_s, l_s, acc_s = carry[h]
        o_s = acc_s / l_s
        sw = sw_h[h]
        mw = jnp.max(sw, axis=0, keepdims=True)
        pw = jnp.exp2(sw - mw)
        lw = jnp.sum(pw, axis=0, keepdims=True)
        pwb = pw.astype(BF)
        o_w = jnp.zeros((HEAD_DIM, wl), F32)
        for j in range(n_piece):
            cj = jnp.maximum(i - WINDOW // LANES + j, 0)
            vt = vwt_ref[0, cj, h * HEAD_DIM:(h + 1) * HEAD_DIM, :]
            o_w = o_w + _dot(vt, pwb[j * LANES:(j + 1) * LANES])
        o_w = o_w / lw

        def gate_row(j):
            return jnp.concatenate(
                [gate_t[(h * NSA_G + g) * 3 + j:(h * NSA_G + g) * 3 + j + 1, :] for g in range(NSA_G)], axis=1)

        o_t = gate_row(0) * oc_h[h] + gate_row(1) * o_s + gate_row(2) * o_w
        for gp in range(NSA_G // 2):
            pr = jnp.concatenate([o_t[:, (2 * gp) * tq:(2 * gp + 1) * tq],
                                  o_t[:, (2 * gp + 1) * tq:(2 * gp + 2) * tq]], axis=0)
            c0 = (h * (NSA_G // 2) + gp) * LANES
            out_ref[:, c0:c0 + LANES] = pr.T * _silu(nz_ref[:, c0:c0 + LANES])


def _nsa_prompt(q, misc, nz, cmp, ksa, vst, kwb, vwt, *, batch, seq):
    tq = NSA_TQ
    nq = seq // tq
    nb = seq // SEL_BLOCK
    assert tq == LANES and WINDOW % LANES == 0 and nb <= LANES
    row = lambda b, i: (b * nq + i, 0)
    return pl.pallas_call(
        functools.partial(_nsa_prompt_body, seq=seq),
        grid=(batch, nq),
        in_specs=[pl.BlockSpec((tq, NSA_WIDTH), row), pl.BlockSpec((tq, LANES), row),
                  pl.BlockSpec((tq, NSA_WIDTH), row),
                  pl.BlockSpec((1, nb, KVW), lambda b, i: (b, 0, 0)),
                  pl.BlockSpec((seq, HALF + LANES), lambda b, i: (b, 0)),
                  pl.BlockSpec((1, seq // PROJ_TM, HALF, PROJ_TM), lambda b, i: (b, 0, 0, 0)),
                  pl.BlockSpec((seq, HALF), lambda b, i: (b, 0)),
                  pl.BlockSpec((1, seq // LANES, HALF, LANES), lambda b, i: (b, 0, 0, 0))],
        out_specs=pl.BlockSpec((tq, NSA_WIDTH), row),
        out_shape=jax.ShapeDtypeStruct((batch * seq, NSA_WIDTH), F32),
        scratch_shapes=[pltpu.VMEM((2, NSA_KV, PROJ_TM, NSA_G * tq), F32)],
        compiler_params=_cparams(2),
        name="nsa_prompt",
    )(q, misc, nz, cmp.reshape(batch, nb, KVW), ksa, vst, kwb, vwt)


def _page_copies(pt_ref, cache_ref, buf_ref, sem_ref, step, *, layer, n_chunks):
    pps = PAGES_PER_STEP
    slot = step % 2
    b = step // n_chunks
    c = step % n_chunks
    return [pltpu.make_async_copy(cache_ref.at[layer, pt_ref[b, c * pps + r]], buf_ref.at[slot, r], sem_ref.at[slot])
            for r in range(pps)]


def _fetch_pages(pt_ref, cache_ref, buf_ref, sem_ref, *, layer, n_chunks):
    step = pl.program_id(0) * n_chunks + pl.program_id(1)
    last = pl.num_programs(0) * n_chunks - 1
    copies = functools.partial(_page_copies, pt_ref, cache_ref, buf_ref, sem_ref, layer=layer, n_chunks=n_chunks)

    @pl.when(step == 0)
    def _():
        for cp in copies(step):
            cp.start()

    @pl.when(step < last)
    def _():
        for cp in copies(step + 1):
            cp.start()

    for cp in copies(step):
        cp.wait()
    return buf_ref.at[step % 2]


def _cmp_past_body(pt_ref, cache_ref, wc_ref, out_ref, buf_ref, sem_ref, *, layer, n_chunks):
    pages = _fetch_pages(pt_ref, cache_ref, buf_ref, sem_ref, layer=layer, n_chunks=n_chunks)
    rows = jnp.concatenate([pages[r].T for r in range(PAGES_PER_STEP)], axis=0)
    blocks = rows.reshape(rows.shape[0] // CMP_BLOCK, CMP_BLOCK, KVW)
    out_ref[0] = jnp.sum(blocks * wc_ref[...][None], axis=1)


def _rows_minor(cache):
    n_layers, n_pool, n_rows = cache.shape[:3]
    return jnp.transpose(cache, (0, 1, 3, 4, 5, 2)).reshape(n_layers, n_pool, KVW, n_rows)


def _cmp_past(cache4, page_table, wc, *, layer):
    page_size = cache4.shape[3]
    bd, n_pages = page_table.shape
    pps = PAGES_PER_STEP
    per_page = page_size // CMP_BLOCK
    n_chunks = n_pages // pps
    grid_spec = pltpu.PrefetchScalarGridSpec(
        num_scalar_prefetch=1,
        grid=(bd, n_chunks),
        in_specs=[pl.BlockSpec(memory_space=pl.ANY), pl.BlockSpec((CMP_BLOCK, KVW), lambda b, c, pt: (0, 0))],
        out_specs=pl.BlockSpec((1, pps * per_page, KVW), lambda b, c, pt: (b, c, 0)),
        scratch_shapes=[pltpu.VMEM((2, pps, KVW, page_size), F32), pltpu.SemaphoreType.DMA((2,))],
    )
    return pl.pallas_call(
        functools.partial(_cmp_past_body, layer=layer, n_chunks=n_chunks),
        grid_spec=grid_spec,
        out_shape=jax.ShapeDtypeStruct((bd, n_pages * per_page, KVW), F32),
        compiler_params=_cparams(2),
        name="cmp_past",
    )(page_table, cache4, wc)


def _nsa_sample_body(pt_ref, cache_ref, q_ref, misc_ref, nz_ref, cmp_ref, kvs_ref, kvw_ref, win_ref, out_ref,
                     newwin_ref, qs_ref, masked_ref, m_ref, l_ref, acc_ref, expand_ref, buf_ref, sem_ref,
                     *, t_new, past_len, page_size, layer):
    pps = PAGES_PER_STEP
    c = pl.program_id(1)
    n_chunks = past_len // (pps * page_size)
    rows = NSA_HEADS * t_new
    nbp = past_len // SEL_BLOCK
    keys = pps * page_size
    bps = keys // SEL_BLOCK
    t_of_row = lax.broadcasted_iota(jnp.int32, (rows, 1), 0) % t_new
    eye = jnp.where(lax.broadcasted_iota(jnp.int32, (rows, rows), 0)
                    == lax.broadcasted_iota(jnp.int32, (rows, rows), 1), 1.0, 0.0).astype(BF)

    @pl.when((pl.program_id(0) == 0) & (c == 0))
    def _():
        eb = lax.broadcasted_iota(jnp.int32, (2 * nbp, keys), 0) - (nbp - bps)
        ek = lax.broadcasted_iota(jnp.int32, (2 * nbp, keys), 1) // SEL_BLOCK
        expand_ref[...] = jnp.where(eb == ek, NEG, 0.0).astype(BF)

    @pl.when(c == 0)
    def _():
        qf = q_ref[...] * (HEAD_DIM ** -0.5)
        lane_head = lax.broadcasted_iota(jnp.int32, (t_new, LANES), 1) // HEAD_DIM
        tiles = []
        for hg in range(NSA_HEADS):
            h = hg // NSA_G
            t = qf[:, (hg // 2) * LANES:(hg // 2 + 1) * LANES]
            if hg % 2 != h:
                t = pltpu.roll(t, HEAD_DIM, axis=1)
            tiles.append(jnp.where(lane_head == h, t, 0.0))
        qs = jnp.concatenate(tiles, axis=0)
        qs_ref[...] = qs
        kc = cmp_ref[0]
        s = _dot_nt(kc[:, :HALF].astype(BF), qs.astype(BF))
        m = jnp.max(s, axis=0, keepdims=True)
        e = jnp.exp(s - m)
        p_c = e / jnp.sum(e, axis=0, keepdims=True)
        p_ct = _dot_nt(eye, p_c.astype(BF)).astype(BF)
        acc_ref[1] = _dot(p_ct, kc[:, HALF:].astype(BF))
        ri = lax.broadcasted_iota(jnp.int32, (rows, rows), 0)
        ci = lax.broadcasted_iota(jnp.int32, (rows, rows), 1)
        same = ((ri // (NSA_G * t_new)) == (ci // (NSA_G * t_new))) & ((ri % t_new) == (ci % t_new))
        gsum = jnp.where(same, 1.0, 0.0).astype(BF)
        p0, p1, p2 = _split3(p_c)
        imp = _dot(p0, gsum) + _dot(p1, gsum) + _dot(p2, gsum)
        blk = lax.broadcasted_iota(jnp.int32, (nbp, rows), 0)
        forced = (blk == 0) | (blk == nbp - 1)
        sel = _select_blocks(imp, forced, blk >= 0, TOP_K - 3)
        masked_ref[...] = _dot_nt(eye, jnp.where(sel, 0.0, 1.0).astype(BF)).astype(BF)
        m_ref[...] = jnp.full(m_ref.shape, NEG, F32)
        l_ref[...] = jnp.zeros(l_ref.shape, F32)
        acc_ref[0] = jnp.zeros(acc_ref.shape[1:], F32)

    qsb = qs_ref[...].astype(BF)
    start = pl.multiple_of((nbp - bps) - c * bps, bps)
    bias = _dot(masked_ref[...], expand_ref[pl.ds(start, nbp), :])
    pages = _fetch_pages(pt_ref, cache_ref, buf_ref, sem_ref, layer=layer, n_chunks=n_chunks)
    kt = jnp.concatenate([pages[r, :HALF, :].astype(BF) for r in range(pps)], axis=1)
    vt = jnp.concatenate([pages[r, HALF:, :].astype(BF) for r in range(pps)], axis=1)
    s = _dot(qsb, kt) + bias
    m_i = m_ref[...]
    m_n = jnp.maximum(m_i, jnp.max(s, axis=1, keepdims=True))
    alpha = jnp.exp(m_i - m_n)
    p = jnp.exp(s - m_n)
    l_ref[...] = alpha * l_ref[...] + jnp.sum(p, axis=1, keepdims=True)
    acc_ref[0] = alpha * acc_ref[0] + _dot_nt(p.astype(BF), vt)
    m_ref[...] = m_n

    @pl.when(c == n_chunks - 1)
    def _():
        kvs = kvs_ref[...]
        tn = lax.broadcasted_iota(jnp.int32, (rows, t_new), 1)
        s_n = jnp.where(tn <= t_of_row, _dot_nt(qsb, kvs[:, :HALF].astype(BF)), NEG)
        m_i = m_ref[...]
        m_n = jnp.maximum(m_i, jnp.max(s_n, axis=1, keepdims=True))
        alpha = jnp.exp(m_i - m_n)
        p_n = jnp.exp(s_n - m_n)
        l_s = alpha * l_ref[...] + jnp.sum(p_n, axis=1, keepdims=True)
        o_s = (alpha * acc_ref[0] + _dot(p_n.astype(BF), kvs[:, HALF:].astype(BF))) / l_s
        wbt = win_ref[0, 0]
        w_buf = wbt.shape[1]
        kvw = kvw_ref[...]
        rb = lax.broadcasted_iota(jnp.int32, (rows, w_buf), 1)
        s_b = jnp.where(w_buf + t_of_row - rb < WINDOW, _dot(qsb, wbt[:HALF].astype(BF)), NEG)
        s_w = jnp.where(tn <= t_of_row, _dot_nt(qsb, kvw[:, :HALF].astype(BF)), NEG)
        m_w = jnp.maximum(jnp.max(s_b, axis=1, keepdims=True), jnp.max(s_w, axis=1, keepdims=True))
        p_b = jnp.exp(s_b - m_w)
        p_w = jnp.exp(s_w - m_w)
        l_w = jnp.sum(p_b, axis=1, keepdims=True) + jnp.sum(p_w, axis=1, keepdims=True)
        o_w = (_dot_nt(p_b.astype(BF), wbt[HALF:].astype(BF)) + _dot(p_w.astype(BF), kvw[:, HALF:].astype(BF))) / l_w
        shifted = pltpu.roll(wbt, w_buf - t_new, axis=1)
        newwin_ref[0, :, :w_buf - LANES] = shifted[:, :w_buf - LANES]
        tail = jnp.concatenate([jnp.zeros((LANES - t_new, KVW), F32), kvw], axis=0).T
        lane = lax.broadcasted_iota(jnp.int32, (KVW, LANES), 1)
        newwin_ref[0, :, w_buf - LANES:] = jnp.where(lane >= LANES - t_new, tail, shifted[:, w_buf - LANES:])
        gates = _sigmoid(misc_ref[...])
        o_c = acc_ref[1]
        nz = nz_ref[...]
        for hg in range(NSA_HEADS):
            h = hg // NSA_G
            r0 = hg * t_new
            o = (gates[:, hg * 3:hg * 3 + 1] * o_c[r0:r0 + t_new]
                 + gates[:, hg * 3 + 1:hg * 3 + 2] * o_s[r0:r0 + t_new]
                 + gates[:, hg * 3 + 2:hg * 3 + 3] * o_w[r0:r0 + t_new])
            c0 = hg * HEAD_DIM
            out_ref[:, c0:c0 + HEAD_DIM] = o[:, h * HEAD_DIM:(h + 1) * HEAD_DIM] * _silu(nz[:, c0:c0 + HEAD_DIM])


def _nsa_sample(q, misc, nz, cmp_past, kvs, kvw, cache4, win4, page_table, *, layer, t_new):
    page_size = cache4.shape[3]
    bd, n_pages = page_table.shape
    past_len = n_pages * page_size
    w_buf = win4.shape[3]
    pps = PAGES_PER_STEP
    nbp = past_len // SEL_BLOCK
    rows = NSA_HEADS * t_new
    row = lambda b, c, pt: (b, 0)
    grid_spec = pltpu.PrefetchScalarGridSpec(
        num_scalar_prefetch=1,
        grid=(bd, n_pages // pps),
        in_specs=[pl.BlockSpec(memory_space=pl.ANY),
                  pl.BlockSpec((t_new, NSA_WIDTH), row), pl.BlockSpec((t_new, LANES), row),
                  pl.BlockSpec((t_new, NSA_WIDTH), row),
                  pl.BlockSpec((1, nbp, KVW), lambda b, c, pt: (b, 0, 0)),
                  pl.BlockSpec((t_new, KVW), row), pl.BlockSpec((t_new, KVW), row),
                  pl.BlockSpec((1, 1, KVW, w_buf), lambda b, c, pt: (layer, b, 0, 0))],
        out_specs=[pl.BlockSpec((t_new, NSA_WIDTH), row),
                   pl.BlockSpec((1, KVW, w_buf), lambda b, c, pt: (b, 0, 0))],
        scratch_shapes=[pltpu.VMEM((rows, LANES), F32), pltpu.VMEM((rows, nbp), BF),
                        pltpu.VMEM((rows, 1), F32), pltpu.VMEM((rows, 1), F32),
                        pltpu.VMEM((2, rows, LANES), F32), pltpu.VMEM((2 * nbp, pps * page_size), BF),
                        pltpu.VMEM((2, pps, KVW, page_size), F32), pltpu.SemaphoreType.DMA((2,))],
    )
    return pl.pallas_call(
        functools.partial(_nsa_sample_body, t_new=t_new, past_len=past_len, page_size=page_size, layer=layer),
        grid_spec=grid_spec,
        out_shape=[jax.ShapeDtypeStruct((bd * t_new, NSA_WIDTH), F32),
                   jax.ShapeDtypeStruct((bd, KVW, w_buf), F32)],
        compiler_params=_cparams(2),
        name="nsa_sample",
    )(page_table, cache4, q, misc, nz, cmp_past, kvs, kvw, win4)


def _gla_body(gla_ref, misc_ref, a2_ref, ab_ref, gn_ref, s0_ref, out_ref, sout_ref, st_ref, *, chunk, mxu_dtype):
    j = pl.program_id(1)
    n_steps = pl.num_programs(1)
    nseq, tb = gla_ref.shape[0], gla_ref.shape[1]
    sub = min(GLA_SUB, chunk)
    hk = GLA_HEADS * chunk
    n_c = tb // chunk
    probs = [(b, c) for b in range(nseq) for c in range(n_c)]

    diag_sv = _gla_diag(GLA_WIDTH, GLA_DV, GLA_KW, GLA_DK)
    diag_k = _gla_diag(hk, chunk, GLA_KW, GLA_DK)
    diag_v = _gla_diag(hk, chunk, GLA_WIDTH, GLA_DV)

    @pl.when(j == 0)
    def _():
        ri = lax.broadcasted_iota(jnp.int32, (GLA_WIDTH, GLA_DV), 0) % GLA_DV
        ci = lax.broadcasted_iota(jnp.int32, (GLA_WIDTH, GLA_DV), 1)
        pick = jnp.where(ri == ci, 1.0, 0.0).astype(BF)
        for b in range(nseq):
            s0 = s0_ref[b].reshape(GLA_KW, GLA_DV)
            wide = sum(_dot_nt(pick, part) for part in _split3(s0))
            st_ref[b] = jnp.where(diag_sv, wide, 0.0)

    tril = jnp.where(lax.broadcasted_iota(jnp.int32, (chunk, chunk), 0)
                     >= lax.broadcasted_iota(jnp.int32, (chunk, chunk), 1), 1.0, 0.0).astype(BF)
    srow = lax.broadcasted_iota(jnp.int32, (chunk, GLA_KW), 0)
    ri = lax.broadcasted_iota(jnp.int32, (GLA_WIDTH, GLA_WIDTH), 0) // GLA_DV
    ci = lax.broadcasted_iota(jnp.int32, (GLA_WIDTH, GLA_WIDTH), 1) // GLA_DV
    head_mean = jnp.where(ri == ci, 1.0 / GLA_DV, 0.0).astype(BF)
    a_col = lax.broadcasted_iota(jnp.int32, (sub, hk), 1) % chunk
    a_row = lax.broadcasted_iota(jnp.int32, (sub, hk), 0)

    def cast(a):
        return a.astype(mxu_dtype)

    def rows(ref, p):
        b, c = p
        return ref[b, c * chunk:(c + 1) * chunk, :]

    a2 = cast(a2_ref[...])
    a_logit = [_dot(cast(rows(misc_ref, p)), a2) + ab_ref[...] for p in probs]
    log_a = [(jnp.minimum(a, 0.0) - jnp.log(1.0 + jnp.exp(-jnp.abs(a)))) * (1.0 / GLA_TAU) for a in a_logit]
    la = [_split3(x) for x in log_a]
    bcum = [_dot(tril, l0) + _dot(tril, l1) + _dot(tril, l2) for (l0, l1, l2) in la]
    gq = [rows(gla_ref, p)[:, 0:GLA_KW] * (GLA_DK ** -0.5) for p in probs]
    gk = [rows(gla_ref, p)[:, GLA_KW:2 * GLA_KW] for p in probs]
    gv = [rows(gla_ref, p)[:, 2 * GLA_KW:2 * GLA_KW + GLA_WIDTH] for p in probs]
    qe = [cast(q * jnp.exp(b_)) for q, b_ in zip(gq, bcum)]
    v_bd = [cast(jnp.where(diag_v, jnp.concatenate([v] * GLA_HEADS, axis=0), 0.0)) for v in gv]
    intra = []
    for i, p in enumerate(probs):
        parts = []
        for sb in range(chunk // sub):
            t0 = sb * sub
            ref_row = bcum[i][t0:t0 + 1, :]
            qd = gq[i][t0:t0 + sub] * jnp.exp(bcum[i][t0:t0 + sub] - ref_row)
            kd = gk[i] * jnp.exp(jnp.where(srow < t0 + sub, ref_row - bcum[i], NEG))
            k_bd = jnp.where(diag_k, jnp.concatenate([kd] * GLA_HEADS, axis=0), 0.0)
            att = _dot_nt(cast(qd), cast(k_bd))
            att = jnp.where(a_col <= a_row + t0, att, 0.0)
            parts.append(_dot(cast(att), v_bd[i]))
        intra.append(jnp.concatenate(parts, axis=0))
    b_last = [b_[chunk - 1:chunk, :] for b_ in bcum]
    upd = [jnp.where(diag_sv, _dot(cast(_transpose_mxu(v, mxu_dtype)), cast(k * jnp.exp(bl - b_))), 0.0)
           for v, k, bl, b_ in zip(gv, gk, b_last, bcum)]
    decay = [jnp.exp(bl) for bl in b_last]
    o = []
    for b in range(nseq):
        st = st_ref[b]
        for c in range(n_c):
            i = b * n_c + c
            o.append(_dot_nt(qe[i], cast(st)) + intra[i])
            st = st * decay[i] + upd[i]
        st_ref[b] = st
    for i, (b, c) in enumerate(probs):
        q0, q1, _ = _split3(o[i] * o[i])
        ms = _dot(q0, head_mean) + _dot(q1, head_mean)
        gz = rows(gla_ref, (b, c))[:, 2 * GLA_KW + GLA_WIDTH:]
        out_ref[b, c * chunk:(c + 1) * chunk, :] = o[i] * lax.rsqrt(ms + EPS) * gn_ref[...] * _silu(gz)

    @pl.when(j == n_steps - 1)
    def _():
        for b in range(nseq):
            st = st_ref[b]
            acc = st[0:GLA_DV]
            for h in range(1, GLA_HEADS):
                acc = acc + st[h * GLA_DV:(h + 1) * GLA_DV]
            sout_ref[b] = _transpose_mxu(acc, F32).reshape(GLA_HEADS, GLA_DK, GLA_DV)


def _gla_diag(n_rows, row_group, n_cols, col_group):
    r = lax.broadcasted_iota(jnp.int32, (n_rows, n_cols), 0) // row_group
    c = lax.broadcasted_iota(jnp.int32, (n_rows, n_cols), 1) // col_group
    return r == c


def _gla(gla, misc, a2p, ab, gn, s0, *, batch, seq):
    chunk = min(GLA_CHUNK, seq)
    tb = min(GLA_TILE, seq)
    nseq = min(batch, GLA_PROBLEMS * chunk // tb)
    steps = seq // tb
    row = lambda g, j: (g, j, 0)
    fixed = lambda g, j: (0, 0)
    state = lambda g, j: (g, 0, 0, 0)
    o_gla, s_new = pl.pallas_call(
        functools.partial(_gla_body, chunk=chunk, mxu_dtype=BF if chunk >= 16 else F32),
        grid=(batch // nseq, steps),
        in_specs=[pl.BlockSpec((nseq, tb, W_GLA), row), pl.BlockSpec((nseq, tb, LANES), row),
                  pl.BlockSpec((LANES, GLA_KW), fixed), pl.BlockSpec((1, GLA_KW), fixed),
                  pl.BlockSpec((1, GLA_WIDTH), fixed),
                  pl.BlockSpec((nseq, GLA_HEADS, GLA_DK, GLA_DV), state)],
        out_specs=[pl.BlockSpec((nseq, tb, GLA_WIDTH), row),
                   pl.BlockSpec((nseq, GLA_HEADS, GLA_DK, GLA_DV), state)],
        out_shape=[jax.ShapeDtypeStruct((batch, seq, GLA_WIDTH), F32),
                   jax.ShapeDtypeStruct((batch, GLA_HEADS, GLA_DK, GLA_DV), F32)],
        scratch_shapes=[pltpu.VMEM((nseq, GLA_WIDTH, GLA_KW), F32)],
        compiler_params=_cparams(2),
        name="gla_prompt" if seq > GLA_CHUNK else "gla_sample",
    )(gla.reshape(batch, seq, W_GLA), misc.reshape(batch, seq, LANES), a2p, ab, gn, s0)
    return o_gla.reshape(batch * seq, GLA_WIDTH), s_new


def _out_body(x_ref, nsa_ref, gla_ref, conv_ref, halo_ref, buf_ref, cw_ref, wo_ref, g_ref, y_ref, nbuf_ref,
              *, seq, tm):
    cv = conv_ref[...]
    cb = cv[:, 0:CONV_WIDTH]
    u = cv[:, CONV_WIDTH:2 * CONV_WIDTH] * cv[:, 2 * CONV_WIDTH:3 * CONV_WIDTH]
    cz = cv[:, 3 * CONV_WIDTH:]
    rows = lax.broadcasted_iota(jnp.int32, (tm, CONV_WIDTH), 0)
    if seq >= tm:
        first = pl.program_id(0) % (seq // tm) == 0
        hv = halo_ref[...]
        hu = hv[:, CONV_WIDTH:2 * CONV_WIDTH] * hv[:, 2 * CONV_WIDTH:3 * CONV_WIDTH]
        bufv = buf_ref[0]
        p1 = jnp.where(first, bufv[1:2], hu[7:8])
        p2 = jnp.where(first, bufv[0:1], hu[6:7])
        prev1 = jnp.where(rows == 0, p1, pltpu.roll(u, 1, axis=0))
        prev2 = jnp.where(rows == 0, p2, jnp.where(rows == 1, p1, pltpu.roll(u, 2, axis=0)))
        nbuf_ref[0] = u[tm - (CONV_K - 1):, :]
    else:
        nseq = tm // seq
        bufv = buf_ref[...]
        b0 = jnp.broadcast_to(bufv[:, 0:1, :], (nseq, seq, CONV_WIDTH)).reshape(tm, CONV_WIDTH)
        b1 = jnp.broadcast_to(bufv[:, 1:2, :], (nseq, seq, CONV_WIDTH)).reshape(tm, CONV_WIDTH)
        t = rows % seq
        prev1 = jnp.where(t == 0, b1, pltpu.roll(u, 1, axis=0))
        prev2 = jnp.where(t == 0, b0, jnp.where(t == 1, b1, pltpu.roll(u, 2, axis=0)))
        nbuf_ref[...] = u.reshape(nseq, seq, CONV_WIDTH)[:, seq - (CONV_K - 1):, :]
    cw = cw_ref[...]
    y = prev2 * cw[0:1] + prev1 * cw[1:2] + u * cw[2:3]
    o_conv = cb * y * _silu(cz)
    acc = _dot(nsa_ref[...].astype(BF), wo_ref[0:NSA_WIDTH, :])
    acc = acc + _dot(gla_ref[...].astype(BF), wo_ref[NSA_WIDTH:NSA_WIDTH + GLA_WIDTH, :])
    acc = acc + _dot(o_conv.astype(BF), wo_ref[NSA_WIDTH + GLA_WIDTH:, :])
    normed = acc * lax.rsqrt(jnp.mean(acc * acc, axis=-1, keepdims=True) + EPS) * g_ref[...]
    y_ref[...] = x_ref[...] + normed


def _out(x2, o_nsa, o_gla, conv, buf, cw, wo, g, *, batch, seq):
    n, d = x2.shape
    tm = PROJ_TM if seq >= PROJ_TM else n
    steps = n // tm
    row = lambda i: (i, 0)
    fixed = lambda i: (0, 0)
    if seq >= tm:
        per_seq = seq // tm
        halo_spec = pl.BlockSpec((8, W_CONV), lambda i: (jnp.maximum(i * (tm // 8) - 1, 0), 0))
        buf_spec = pl.BlockSpec((1, CONV_K - 1, CONV_WIDTH), lambda i: (i // per_seq, 0, 0))
    else:
        halo_spec = pl.BlockSpec((8, W_CONV), fixed)
        buf_spec = pl.BlockSpec((batch, CONV_K - 1, CONV_WIDTH), lambda i: (0, 0, 0))
    return pl.pallas_call(
        functools.partial(_out_body, seq=seq, tm=tm),
        grid=(steps,),
        in_specs=[pl.BlockSpec((tm, d), row), pl.BlockSpec((tm, NSA_WIDTH), row),
                  pl.BlockSpec((tm, GLA_WIDTH), row), pl.BlockSpec((tm, W_CONV), row),
                  halo_spec, buf_spec,
                  pl.BlockSpec((CONV_K, CONV_WIDTH), fixed), pl.BlockSpec((d, d), fixed),
                  pl.BlockSpec((1, d), fixed)],
        out_specs=[pl.BlockSpec((tm, d), row), buf_spec],
        out_shape=[jax.ShapeDtypeStruct((n, d), F32),
                   jax.ShapeDtypeStruct((batch, CONV_K - 1, CONV_WIDTH), F32)],
        compiler_params=_cparams(1),
        name="out_prompt" if seq >= PROJ_TM else "out_sample",
    )(x2, o_nsa, o_gla, conv, conv, buf, cw, wo, g)


def _reorder_w_in(w):
    o_gate = NSA_WIDTH + 3 * KVW
    o_nz = o_gate + N_GATE
    o_gla = o_nz + NSA_WIDTH
    o_a = o_gla + 2 * GLA_KW + GLA_WIDTH
    o_gz = o_a + GLA_RANK
    o_conv = o_gz + GLA_WIDTH
    pad = jnp.zeros((w.shape[0], LANES - N_GATE - GLA_RANK), w.dtype)
    return jnp.concatenate([w[:, :o_gate], w[:, o_nz:o_gla], w[:, o_gla:o_a], w[:, o_gz:o_conv], w[:, o_conv:],
                            w[:, o_gate:o_nz], w[:, o_a:o_gz], pad], axis=1)


def _layer(x, l, P, *, prompt, cache_cmp_kv=None, cache_slc_kv=None, cache_win_kv=None, page_table=None,
           gla_s0=None, conv_buf=None):
    batch, seq, d = x.shape
    x2 = x.reshape(batch * seq, d)
    outs = _proj(x2, P['norm_pre'][l], P['w_in'][l], P['w_cmp'][l], batch=batch, seq=seq, prompt=prompt)
    q, kvc, kvs, kvw, nz, gla, conv, misc = outs[:8]
    kv_shape = (batch, seq, 2, NSA_KV, HEAD_DIM)
    if prompt:
        ksb, kwb, vst, vwt, cmp = outs[8:]
        o_nsa = _nsa_prompt(q, misc, nz, cmp, ksb, vst, kwb, vwt, batch=batch, seq=seq)
        w_keep = min(WINDOW, seq)
        new_w = kvw.reshape(kv_shape)[:, seq - w_keep:]
    else:
        cmp_past = _cmp_past(cache_cmp_kv, page_table, P['w_cmp'][l], layer=l)
        o_nsa, new_w = _nsa_sample(q, misc, nz, cmp_past, kvs, kvw, cache_slc_kv, cache_win_kv, page_table,
                                   layer=l, t_new=seq)
        w_buf = new_w.shape[2]
        new_w = jnp.transpose(new_w.reshape((batch,) + kv_shape[2:] + (w_buf,)), (0, 4, 1, 2, 3))
    o_gla, s_gla = _gla(gla, misc, P['gla_a2'][l], P['gla_ab'][l], P['gla_norm'][l], gla_s0, batch=batch, seq=seq)
    y2, new_buf = _out(x2, o_nsa, o_gla, conv, conv_buf, P['conv_w'][l], P['w_out'][l], P['norm_post'][l],
                       batch=batch, seq=seq)
    return y2.reshape(batch, seq, d), (kvc.reshape(kv_shape), kvs.reshape(kv_shape), new_w, s_gla, new_buf)


def kernel(x_prompt, x_sample, cache_cmp_kv, cache_slc_kv, cache_win_kv, state_gla, state_conv, page_table,
           norm_pre, norm_post, w_in, w_out, w_cmp_k, w_cmp_v, gla_a2, gla_ab, gla_norm, conv_w):
    depth = w_in.shape[0]
    d = w_in.shape[1]
    ones = jnp.ones((1, HALF), F32)
    a2p = jnp.zeros((depth, LANES, GLA_KW), F32).at[:, N_GATE:N_GATE + GLA_RANK, :].set(gla_a2)
    P = {
        'norm_pre': norm_pre.reshape(depth, 1, d),
        'norm_post': norm_post.reshape(depth, 1, d),
        'w_in': jax.vmap(_reorder_w_in)(w_in).astype(BF),
        'w_out': w_out.astype(BF),
        'w_cmp': jnp.concatenate([w_cmp_k[:, :, None] * ones, w_cmp_v[:, :, None] * ones], axis=2),
        'gla_a2': a2p,
        'gla_ab': gla_ab.reshape(depth, 1, GLA_KW),
        'gla_norm': jnp.tile(gla_norm, (1, GLA_HEADS)).reshape(depth, 1, GLA_WIDTH),
        'conv_w': conv_w,
    }
    xp, xs = x_prompt, x_sample
    bp = xp.shape[0]
    st_p, st_s = [], []
    cache_cmp_kv, cache_slc_kv, cache_win_kv = (_rows_minor(c) for c in (cache_cmp_kv, cache_slc_kv, cache_win_kv))
    for l in range(depth):
        xp, sp = _layer(xp, l, P, prompt=True,
                        gla_s0=jnp.zeros((bp, GLA_HEADS, GLA_DK, GLA_DV), F32),
                        conv_buf=jnp.zeros((bp, CONV_K - 1, CONV_WIDTH), F32))
        xs, ss = _layer(xs, l, P, prompt=False, cache_cmp_kv=cache_cmp_kv, cache_slc_kv=cache_slc_kv,
                        cache_win_kv=cache_win_kv, page_table=page_table,
                        gla_s0=state_gla[l], conv_buf=state_conv[l])
        st_p.append(sp)
        st_s.append(ss)
    stack = lambda sts, k: jnp.stack([s[k] for s in sts])
    return (xp, xs, stack(st_p, 0), stack(st_s, 0), stack(st_p, 1), stack(st_s, 1), stack(st_p, 2), stack(st_s, 2),
            stack(st_p, 3), stack(st_s, 3), stack(st_p, 4), stack(st_s, 4))
```

```python
import functools

import jax
import jax.numpy as jnp
from jax import lax
from jax.experimental import pallas as pl
from jax.experimental.pallas import tpu as pltpu

F32 = jnp.float32
BF = jnp.bfloat16

HEAD_DIM = 64
NSA_KV = 2
NSA_G = 4
NSA_HEADS = NSA_KV * NSA_G
NSA_WIDTH = NSA_HEADS * HEAD_DIM
KVW = 2 * NSA_KV * HEAD_DIM
HALF = NSA_KV * HEAD_DIM
CMP_BLOCK = 64
SEL_BLOCK = 64
TOP_K = 16
WINDOW = 512
GLA_HEADS = 4
GLA_DK = 32
GLA_DV = 64
GLA_KW = GLA_HEADS * GLA_DK
GLA_WIDTH = GLA_HEADS * GLA_DV
GLA_RANK = 16
GLA_TAU = 16.0
GLA_CHUNK = 64
GLA_SUB = 16
GLA_TILE = 256
GLA_PROBLEMS = 8
CONV_WIDTH = 256
CONV_K = 3
N_GATE = 3 * NSA_HEADS
EPS = 1e-6
NEG = -1e30
LOG2E = 1.4426950408889634

LANES = 128
VMEM_LIMIT = 48 * 1024 * 1024

C_Q = 0
C_CMP = C_Q + NSA_WIDTH
C_SLC = C_CMP + KVW
C_WIN = C_SLC + KVW
C_NZ = C_WIN + KVW
C_GLA = C_NZ + NSA_WIDTH
W_GLA = 2 * GLA_KW + 2 * GLA_WIDTH
C_CONV = C_GLA + W_GLA
W_CONV = 4 * CONV_WIDTH
C_MISC = C_CONV + W_CONV
PROJ_PAD = C_MISC + LANES

PROJ_TM = 512
NSA_TQ = 128
PAGES_PER_STEP = 32
PAGE_SLOTS = 3


def _cparams(n_axes, vmem=VMEM_LIMIT):
    return pltpu.CompilerParams(dimension_semantics=("arbitrary",) * n_axes, vmem_limit_bytes=vmem)


def _dot(a, b):
    return jnp.dot(a, b, preferred_element_type=F32)


def _dot_nt(a, b):
    return lax.dot_general(a, b, (((1,), (1,)), ((), ())), preferred_element_type=F32)


def _split3(a):
    a0 = a.astype(BF)
    r = a - a0.astype(F32)
    a1 = r.astype(BF)
    a2 = (r - a1.astype(F32)).astype(BF)
    return a0, a1, a2


def _transpose_mxu(y, dtype):
    n = y.shape[1]
    eye = jnp.where(lax.broadcasted_iota(jnp.int32, (n, n), 0) == lax.broadcasted_iota(jnp.int32, (n, n), 1),
                    1.0, 0.0).astype(BF)
    if dtype == BF:
        return _dot_nt(eye, y.astype(BF))
    return sum(_dot_nt(eye, part) for part in _split3(y))


def _silu(x):
    return x * (1.0 / (1.0 + jnp.exp(-x)))


def _sigmoid(x):
    return 1.0 / (1.0 + jnp.exp(-x))


def _proj_body(x_ref, g_ref, w_ref, wc_ref, *outs, prompt, seq):
    x = x_ref[...]
    h = x * lax.rsqrt(jnp.mean(x * x, axis=-1, keepdims=True) + EPS) * g_ref[...]
    hb = h.astype(BF)

    def proj(c0, n):
        return _dot(hb, w_ref[:, c0:c0 + n])

    q_ref, kvc_ref, kvs_ref, kvw_ref, nz_ref, gla_ref, conv_ref, misc_ref = outs[:8]
    q_ref[...] = proj(C_Q, NSA_WIDTH)
    kvc = proj(C_CMP, KVW)
    kvs = proj(C_SLC, KVW)
    kvw = proj(C_WIN, KVW)
    kvc_ref[...] = kvc
    kvs_ref[...] = kvs
    kvw_ref[...] = kvw
    nz_ref[...] = proj(C_NZ, NSA_WIDTH)
    gla_ref[...] = proj(C_GLA, W_GLA)
    conv_ref[...] = proj(C_CONV, W_CONV)
    misc_ref[...] = proj(C_MISC, LANES)
    if prompt:
        ksb_ref, kwb_ref, vst_ref, vwt_ref, cmp_ref = outs[8:]
        tm = x.shape[0]
        blk = (pl.program_id(0) % (seq // tm)) * (tm // SEL_BLOCK) \
            + lax.broadcasted_iota(jnp.int32, (tm, LANES), 0) // SEL_BLOCK
        onehot = jnp.where(lax.broadcasted_iota(jnp.int32, (tm, LANES), 1) == blk, 1.0, 0.0)
        ksb_ref[...] = jnp.concatenate([kvs[:, :HALF], onehot], axis=1).astype(BF)
        kwb_ref[...] = kvw[:, :HALF].astype(BF)
        vst_ref[0, 0] = kvs[:, HALF:].T.astype(BF)
        for j in range(tm // LANES):
            vwt_ref[0, j] = kvw[j * LANES:(j + 1) * LANES, HALF:].T.astype(BF)
        cmp_ref[...] = jnp.sum(kvc.reshape(tm // CMP_BLOCK, CMP_BLOCK, KVW) * wc_ref[...][None], axis=1)


def _proj(x2, g, w, wc, *, batch, seq, prompt):
    n, d = x2.shape
    tm = PROJ_TM if prompt else n
    steps = n // tm
    per_seq = seq // tm if prompt else 1
    row = lambda i: (i, 0)
    fixed = lambda i: (0, 0)
    widths = [NSA_WIDTH, KVW, KVW, KVW, NSA_WIDTH, W_GLA, W_CONV, LANES]
    out_shape = [jax.ShapeDtypeStruct((n, c), F32) for c in widths]
    out_specs = [pl.BlockSpec((tm, c), row) for c in widths]
    if prompt:
        out_shape += [jax.ShapeDtypeStruct((n, HALF + LANES), BF), jax.ShapeDtypeStruct((n, HALF), BF),
                      jax.ShapeDtypeStruct((batch, seq // tm, HALF, tm), BF),
                      jax.ShapeDtypeStruct((batch, seq // LANES, HALF, LANES), BF),
                      jax.ShapeDtypeStruct((n // CMP_BLOCK, KVW), F32)]
        out_specs += [pl.BlockSpec((tm, HALF + LANES), row), pl.BlockSpec((tm, HALF), row),
                      pl.BlockSpec((1, 1, HALF, tm), lambda i: (i // per_seq, i % per_seq, 0, 0)),
                      pl.BlockSpec((1, tm // LANES, HALF, LANES), lambda i: (i // per_seq, i % per_seq, 0, 0)),
                      pl.BlockSpec((tm // CMP_BLOCK, KVW), row)]
    return pl.pallas_call(
        functools.partial(_proj_body, prompt=prompt, seq=seq),
        grid=(steps,),
        in_specs=[pl.BlockSpec((tm, d), row), pl.BlockSpec((1, d), fixed),
                  pl.BlockSpec((d, PROJ_PAD), fixed), pl.BlockSpec((CMP_BLOCK, KVW), fixed)],
        out_specs=out_specs,
        out_shape=out_shape,
        compiler_params=_cparams(1),
        name="proj_prompt" if prompt else "proj_sample",
    )(x2, g, w, wc)


def _select_blocks(imp, forced, visible, n_pick):
    nb = imp.shape[0]
    blk = lax.broadcasted_iota(jnp.int32, imp.shape, 0)
    score = jnp.where(forced, -jnp.inf, jnp.where(visible, imp, -1.0))
    sel = forced
    for _ in range(n_pick):
        m = jnp.max(score, axis=0, keepdims=True)
        first = jnp.min(jnp.where(score == m, blk, nb), axis=0, keepdims=True)
        hit = blk == first
        sel = sel | hit
        score = jnp.where(hit, -jnp.inf, score)
    return sel


def _nsa_prompt_body(q_ref, misc_ref, nz_ref, cmp_ref, ks_ref, vst_ref, kw_ref, vwt_ref, out_ref, s_ref, *, seq):
    tq = NSA_TQ
    tk = PROJ_TM
    nb = seq // SEL_BLOCK
    n_chunks = seq // tk
    wl = NSA_G * tq
    n_piece = WINDOW // LANES + 1
    i = pl.program_id(1)
    q0 = i * tq
    qf = q_ref[...] * (HEAD_DIM ** -0.5 * LOG2E)
    gate_t = _sigmoid(misc_ref[...]).T
    lane_head = lax.broadcasted_iota(jnp.int32, (tq, LANES), 1) // HEAD_DIM
    kc = cmp_ref[0]
    kcb = kc[:, :HALF].astype(BF)
    vct = kc[:, HALF:].T.astype(BF)

    ql = lax.broadcasted_iota(jnp.int32, (1, wl), 1) % tq
    qpos = q0 + ql
    blk_w = lax.broadcasted_iota(jnp.int32, (nb, wl), 0)
    vis_c = blk_w * CMP_BLOCK + (CMP_BLOCK - 1) <= qpos
    blk2 = lax.broadcasted_iota(jnp.int32, (nb, NSA_KV * tq), 0)
    cur = (q0 + lax.broadcasted_iota(jnp.int32, (nb, NSA_KV * tq), 1) % tq) // SEL_BLOCK
    forced = (blk2 == 0) | (blk2 == cur) | (blk2 == cur - 1)
    visible = blk2 <= cur
    row_p = lax.broadcasted_iota(jnp.int32, (LANES, wl), 0)
    near_ok = row_p > ql
    diag_ok = row_p <= ql

    qs_h, oc_h, imp_h = [], [], []
    for h in range(NSA_KV):
        tiles = []
        for g in range(NSA_G):
            c = h * NSA_G + g
            t = qf[:, (c // 2) * LANES:(c // 2 + 1) * LANES]
            if c % 2 != h:
                t = pltpu.roll(t, HEAD_DIM, axis=1)
            tiles.append(jnp.where(lane_head == h, t, 0.0))
        qs = jnp.concatenate(tiles, axis=0).astype(BF)
        qs_h.append(qs)

        s = jnp.where(vis_c, _dot_nt(kcb, qs), NEG)
        m = jnp.max(s, axis=0, keepdims=True)
        e = jnp.where(vis_c, jnp.exp2(s - m), 0.0)
        l = jnp.sum(e, axis=0, keepdims=True)
        p_c = e / jnp.where(l > 0.0, l, 1.0)
        oc_h.append(_dot(vct[h * HEAD_DIM:(h + 1) * HEAD_DIM], p_c.astype(BF)))
        imp = p_c[:, 0:tq]
        for g in range(1, NSA_G):
            imp = imp + p_c[:, g * tq:(g + 1) * tq]
        imp_h.append(imp)

    sel = _select_blocks(jnp.concatenate(imp_h, axis=1), forced, visible, TOP_K - 3)
    bias = jnp.where(sel, 0.0, NEG)
    if nb < LANES:
        bias = jnp.concatenate([bias, jnp.zeros((LANES - nb, NSA_KV * tq), F32)], axis=0)
    qa_h = []
    for h in range(NSA_KV):
        bias_t = bias[:, h * tq:(h + 1) * tq].T.astype(BF)
        qa_h.append(jnp.concatenate([qs_h[h], jnp.concatenate([bias_t] * NSA_G, axis=0)], axis=1))

    sw_h = []
    for h in range(NSA_KV):
        s_parts = []
        for j in range(n_piece):
            start = q0 - WINDOW + j * LANES
            cl = pl.multiple_of(jnp.maximum(start, 0), LANES)
            sw = _dot_nt(kw_ref[pl.ds(cl, LANES), :], qs_h[h])
            if j == 0:
                sw = jnp.where(near_ok, sw, NEG)
            if j == n_piece - 1:
                sw = jnp.where(diag_ok, sw, NEG)
            else:
                sw = jnp.where(start >= 0, sw, NEG)
            s_parts.append(sw)
        sw_h.append(jnp.concatenate(s_parts, axis=0))

    def put_scores(c, slot):
        ka = ks_ref[pl.ds(pl.multiple_of(c * tk, tk), tk), :]
        for h in range(NSA_KV):
            s_ref[slot, h] = _dot_nt(ka, qa_h[h])

    def softmax_pv(c, slot, stats, causal):
        new, probs = [], []
        for h in range(NSA_KV):
            m_i, l_i, _ = stats[h]
            s = s_ref[slot, h]
            if causal:
                kpos = c * tk + lax.broadcasted_iota(jnp.int32, (tk, wl), 0)
                s = jnp.where(kpos <= qpos, s, NEG)
            m_n = jnp.maximum(m_i, jnp.max(s, axis=0, keepdims=True))
            alpha = jnp.exp2(m_i - m_n)
            p = jnp.exp2(s - m_n)
            new.append((m_n, alpha * l_i + jnp.sum(p, axis=0, keepdims=True), alpha))
            probs.append(p.astype(BF))
        out = []
        for h in range(NSA_KV):
            m_n, l_n, alpha = new[h]
            vt = vst_ref[0, c, h * HEAD_DIM:(h + 1) * HEAD_DIM, :]
            out.append((m_n, l_n, alpha * stats[h][2] + _dot(vt, probs[h])))
        return tuple(out)

    def pair(p, stats):
        put_scores(2 * p + 1, 1)
        stats = softmax_pv(2 * p, 0, stats, False)
        put_scores(2 * p + 2, 0)
        return softmax_pv(2 * p + 1, 1, stats, False)

    init = (jnp.full((1, wl), NEG, F32), jnp.zeros((1, wl), F32), jnp.zeros((HEAD_DIM, wl), F32))
    n_full = q0 // tk
    n_pair = n_full // 2
    put_scores(0, 0)
    stats = lax.fori_loop(0, n_pair, pair, (init,) * NSA_KV)
    put_scores(jnp.minimum(2 * n_pair + 1, n_chunks - 1), 1)
    stats = softmax_pv(2 * n_pair, 0, stats, True)
    carry = lax.cond(n_full % 2 == 1, lambda st: softmax_pv(n_full, 1, st, True), lambda st: st, stats)

    for h in range(NSA_KV):
        m_s, l_s, acc_s = carry[h]
        o_s = acc_s / l_s
        sw = sw_h[h]
        mw = jnp.max(sw, axis=0, keepdims=True)
        pw = jnp.exp2(sw - mw)
        lw = jnp.sum(pw, axis=0, keepdims=True)
        pwb = pw.astype(BF)
        o_w = jnp.zeros((HEAD_DIM, wl), F32)
        for j in range(n_piece):
            cj = jnp.maximum(i - WINDOW // LANES + j, 0)
            vt = vwt_ref[0, cj, h * HEAD_DIM:(h + 1) * HEAD_DIM, :]
            o_w = o_w + _dot(vt, pwb[j * LANES:(j + 1) * LANES])
        o_w = o_w / lw

        def gate_row(j):
            return jnp.concatenate(
                [gate_t[(h * NSA_G + g) * 3 + j:(h * NSA_G + g) * 3 + j + 1, :] for g in range(NSA_G)], axis=1)

        o_t = gate_row(0) * oc_h[h] + gate_row(1) * o_s + gate_row(2) * o_w
        for gp in range(NSA_G // 2):
            pr = jnp.concatenate([o_t[:, (2 * gp) * tq:(2 * gp + 1) * tq],
                                  o_t[:, (2 * gp + 1) * tq:(2 * gp + 2) * tq]], axis=0)
            c0 = (h * (NSA_G // 2) + gp) * LANES
            out_ref[:, c0:c0 + LANES] = pr.T * _silu(nz_ref[:, c0:c0 + LANES])


def _nsa_prompt(q, misc, nz, cmp, ksa, vst, kwb, vwt, *, batch, seq):
    tq = NSA_TQ
    nq = seq // tq
    nb = seq // SEL_BLOCK
    assert tq == LANES and WINDOW % LANES == 0 and nb <= LANES
    row = lambda b, i: (b * nq + i, 0)
    return pl.pallas_call(
        functools.partial(_nsa_prompt_body, seq=seq),
        grid=(batch, nq),
        in_specs=[pl.BlockSpec((tq, NSA_WIDTH), row), pl.BlockSpec((tq, LANES), row),
                  pl.BlockSpec((tq, NSA_WIDTH), row),
                  pl.BlockSpec((1, nb, KVW), lambda b, i: (b, 0, 0)),
                  pl.BlockSpec((seq, HALF + LANES), lambda b, i: (b, 0)),
                  pl.BlockSpec((1, seq // PROJ_TM, HALF, PROJ_TM), lambda b, i: (b, 0, 0, 0)),
                  pl.BlockSpec((seq, HALF), lambda b, i: (b, 0)),
                  pl.BlockSpec((1, seq // LANES, HALF, LANES), lambda b, i: (b, 0, 0, 0))],
        out_specs=pl.BlockSpec((tq, NSA_WIDTH), row),
        out_shape=jax.ShapeDtypeStruct((batch * seq, NSA_WIDTH), F32),
        scratch_shapes=[pltpu.VMEM((2, NSA_KV, PROJ_TM, NSA_G * tq), F32)],
        compiler_params=_cparams(2),
        name="nsa_prompt",
    )(q, misc, nz, cmp.reshape(batch, nb, KVW), ksa, vst, kwb, vwt)


def _page_copies(pt_ref, cache_ref, buf_ref, sem_ref, step, *, layer, n_chunks):
    pps = PAGES_PER_STEP
    slot = step % PAGE_SLOTS
    b = step // n_chunks
    c = step % n_chunks
    return [pltpu.make_async_copy(cache_ref.at[layer, pt_ref[b, c * pps + r]], buf_ref.at[slot, r], sem_ref.at[slot])
            for r in range(pps)]


def _fetch_pages(pt_ref, cache_ref, buf_ref, sem_ref, *, layer, n_chunks):
    step = pl.program_id(0) * n_chunks + pl.program_id(1)
    last = pl.num_programs(0) * n_chunks - 1
    ahead = PAGE_SLOTS - 1
    copies = functools.partial(_page_copies, pt_ref, cache_ref, buf_ref, sem_ref, layer=layer, n_chunks=n_chunks)

    @pl.when(step == 0)
    def _():
        for d in range(ahead):
            @pl.when(d <= last)
            def _():
                for cp in copies(step + d):
                    cp.start()

    @pl.when(step + ahead <= last)
    def _():
        for cp in copies(step + ahead):
            cp.start()

    for cp in copies(step):
        cp.wait()
    return buf_ref.at[step % PAGE_SLOTS]


def _cmp_past_body(pt_ref, cache_ref, wc_ref, out_ref, buf_ref, sem_ref, *, layer, n_chunks):
    pages = _fetch_pages(pt_ref, cache_ref, buf_ref, sem_ref, layer=layer, n_chunks=n_chunks)
    rows = jnp.concatenate([pages[r].T for r in range(PAGES_PER_STEP)], axis=0)
    blocks = rows.reshape(rows.shape[0] // CMP_BLOCK, CMP_BLOCK, KVW)
    out_ref[0] = jnp.sum(blocks * wc_ref[...][None], axis=1)


def _rows_minor(cache):
    n_layers, n_pool, n_rows = cache.shape[:3]
    return jnp.transpose(cache, (0, 1, 3, 4, 5, 2)).reshape(n_layers, n_pool, KVW, n_rows)


def _cmp_past(cache4, page_table, wc, *, layer):
    page_size = cache4.shape[3]
    bd, n_pages = page_table.shape
    pps = PAGES_PER_STEP
    per_page = page_size // CMP_BLOCK
    n_chunks = n_pages // pps
    grid_spec = pltpu.PrefetchScalarGridSpec(
        num_scalar_prefetch=1,
        grid=(bd, n_chunks),
        in_specs=[pl.BlockSpec(memory_space=pl.ANY), pl.BlockSpec((CMP_BLOCK, KVW), lambda b, c, pt: (0, 0))],
        out_specs=pl.BlockSpec((1, pps * per_page, KVW), lambda b, c, pt: (b, c, 0)),
        scratch_shapes=[pltpu.VMEM((PAGE_SLOTS, pps, KVW, page_size), F32), pltpu.SemaphoreType.DMA((PAGE_SLOTS,))],
    )
    return pl.pallas_call(
        functools.partial(_cmp_past_body, layer=layer, n_chunks=n_chunks),
        grid_spec=grid_spec,
        out_shape=jax.ShapeDtypeStruct((bd, n_pages * per_page, KVW), F32),
        compiler_params=_cparams(2),
        name="cmp_past",
    )(page_table, cache4, wc)


def _nsa_sample_body(pt_ref, cache_ref, q_ref, misc_ref, nz_ref, cmp_ref, kvs_ref, kvw_ref, win_ref, out_ref,
                     newwin_ref, qs_ref, masked_ref, m_ref, l_ref, acc_ref, expand_ref, buf_ref, sem_ref,
                     *, t_new, past_len, page_size, layer):
    pps = PAGES_PER_STEP
    c = pl.program_id(1)
    n_chunks = past_len // (pps * page_size)
    rows = NSA_HEADS * t_new
    nbp = past_len // SEL_BLOCK
    keys = pps * page_size
    bps = keys // SEL_BLOCK
    t_of_row = lax.broadcasted_iota(jnp.int32, (rows, 1), 0) % t_new
    eye = jnp.where(lax.broadcasted_iota(jnp.int32, (rows, rows), 0)
                    == lax.broadcasted_iota(jnp.int32, (rows, rows), 1), 1.0, 0.0).astype(BF)

    @pl.when((pl.program_id(0) == 0) & (c == 0))
    def _():
        eb = lax.broadcasted_iota(jnp.int32, (2 * nbp, keys), 0) - (nbp - bps)
        ek = lax.broadcasted_iota(jnp.int32, (2 * nbp, keys), 1) // SEL_BLOCK
        expand_ref[...] = jnp.where(eb == ek, NEG, 0.0).astype(BF)

    @pl.when(c == 0)
    def _():
        qf = q_ref[...] * (HEAD_DIM ** -0.5)
        lane_head = lax.broadcasted_iota(jnp.int32, (t_new, LANES), 1) // HEAD_DIM
        tiles = []
        for hg in range(NSA_HEADS):
            h = hg // NSA_G
            t = qf[:, (hg // 2) * LANES:(hg // 2 + 1) * LANES]
            if hg % 2 != h:
                t = pltpu.roll(t, HEAD_DIM, axis=1)
            tiles.append(jnp.where(lane_head == h, t, 0.0))
        qs = jnp.concatenate(tiles, axis=0)
        qs_ref[...] = qs
        kc = cmp_ref[0]
        s = _dot_nt(kc[:, :HALF].astype(BF), qs.astype(BF))
        m = jnp.max(s, axis=0, keepdims=True)
        e = jnp.exp(s - m)
        p_c = e / jnp.sum(e, axis=0, keepdims=True)
        p_ct = _dot_nt(eye, p_c.astype(BF)).astype(BF)
        acc_ref[1] = _dot(p_ct, kc[:, HALF:].astype(BF))
        ri = lax.broadcasted_iota(jnp.int32, (rows, rows), 0)
        ci = lax.broadcasted_iota(jnp.int32, (rows, rows), 1)
        same = ((ri // (NSA_G * t_new)) == (ci // (NSA_G * t_new))) & ((ri % t_new) == (ci % t_new))
        gsum = jnp.where(same, 1.0, 0.0).astype(BF)
        p0, p1, p2 = _split3(p_c)
        imp = _dot(p0, gsum) + _dot(p1, gsum) + _dot(p2, gsum)
        blk = lax.broadcasted_iota(jnp.int32, (nbp, rows), 0)
        forced = (blk == 0) | (blk == nbp - 1)
        sel = _select_blocks(imp, forced, blk >= 0, TOP_K - 3)
        masked_ref[...] = _dot_nt(eye, jnp.where(sel, 0.0, 1.0).astype(BF)).astype(BF)
        m_ref[...] = jnp.full(m_ref.shape, NEG, F32)
        l_ref[...] = jnp.zeros(l_ref.shape, F32)
        acc_ref[0] = jnp.zeros(acc_ref.shape[1:], F32)

    qsb = qs_ref[...].astype(BF)
    start = pl.multiple_of((nbp - bps) - c * bps, bps)
    bias = _dot(masked_ref[...], expand_ref[pl.ds(start, nbp), :])
    pages = _fetch_pages(pt_ref, cache_ref, buf_ref, sem_ref, layer=layer, n_chunks=n_chunks)
    kt = jnp.concatenate([pages[r, :HALF, :].astype(BF) for r in range(pps)], axis=1)
    vt = jnp.concatenate([pages[r, HALF:, :].astype(BF) for r in range(pps)], axis=1)
    s = _dot(qsb, kt) + bias
    m_i = m_ref[...]
    m_n = jnp.maximum(m_i, jnp.max(s, axis=1, keepdims=True))
    alpha = jnp.exp(m_i - m_n)
    p = jnp.exp(s - m_n)
    l_ref[...] = alpha * l_ref[...] + jnp.sum(p, axis=1, keepdims=True)
    acc_ref[0] = alpha * acc_ref[0] + _dot_nt(p.astype(BF), vt)
    m_ref[...] = m_n

    @pl.when(c == n_chunks - 1)
    def _():
        kvs = kvs_ref[...]
        tn = lax.broadcasted_iota(jnp.int32, (rows, t_new), 1)
        s_n = jnp.where(tn <= t_of_row, _dot_nt(qsb, kvs[:, :HALF].astype(BF)), NEG)
        m_i = m_ref[...]
        m_n = jnp.maximum(m_i, jnp.max(s_n, axis=1, keepdims=True))
        alpha = jnp.exp(m_i - m_n)
        p_n = jnp.exp(s_n - m_n)
        l_s = alpha * l_ref[...] + jnp.sum(p_n, axis=1, keepdims=True)
        o_s = (alpha * acc_ref[0] + _dot(p_n.astype(BF), kvs[:, HALF:].astype(BF))) / l_s
        wbt = win_ref[0, 0]
        w_buf = wbt.shape[1]
        kvw = kvw_ref[...]
        rb = lax.broadcasted_iota(jnp.int32, (rows, w_buf), 1)
        s_b = jnp.where(w_buf + t_of_row - rb < WINDOW, _dot(qsb, wbt[:HALF].astype(BF)), NEG)
        s_w = jnp.where(tn <= t_of_row, _dot_nt(qsb, kvw[:, :HALF].astype(BF)), NEG)
        m_w = jnp.maximum(jnp.max(s_b, axis=1, keepdims=True), jnp.max(s_w, axis=1, keepdims=True))
        p_b = jnp.exp(s_b - m_w)
        p_w = jnp.exp(s_w - m_w)
        l_w = jnp.sum(p_b, axis=1, keepdims=True) + jnp.sum(p_w, axis=1, keepdims=True)
        o_w = (_dot_nt(p_b.astype(BF), wbt[HALF:].astype(BF)) + _dot(p_w.astype(BF), kvw[:, HALF:].astype(BF))) / l_w
        shifted = pltpu.roll(wbt, w_buf - t_new, axis=1)
        newwin_ref[0, :, :w_buf - LANES] = shifted[:, :w_buf - LANES]
        tail = jnp.concatenate([jnp.zeros((LANES - t_new, KVW), F32), kvw], axis=0).T
        lane = lax.broadcasted_iota(jnp.int32, (KVW, LANES), 1)
        newwin_ref[0, :, w_buf - LANES:] = jnp.where(lane >= LANES - t_new, tail, shifted[:, w_buf - LANES:])
        gates = _sigmoid(misc_ref[...])
        o_c = acc_ref[1]
        nz = nz_ref[...]
        for hg in range(NSA_HEADS):
            h = hg // NSA_G
            r0 = hg * t_new
            o = (gates[:, hg * 3:hg * 3 + 1] * o_c[r0:r0 + t_new]
                 + gates[:, hg * 3 + 1:hg * 3 + 2] * o_s[r0:r0 + t_new]
                 + gates[:, hg * 3 + 2:hg * 3 + 3] * o_w[r0:r0 + t_new])
            c0 = hg * HEAD_DIM
            out_ref[:, c0:c0 + HEAD_DIM] = o[:, h * HEAD_DIM:(h + 1) * HEAD_DIM] * _silu(nz[:, c0:c0 + HEAD_DIM])


def _nsa_sample(q, misc, nz, cmp_past, kvs, kvw, cache4, win4, page_table, *, layer, t_new):
    page_size = cache4.shape[3]
    bd, n_pages = page_table.shape
    past_len = n_pages * page_size
    w_buf = win4.shape[3]
    pps = PAGES_PER_STEP
    nbp = past_len // SEL_BLOCK
    rows = NSA_HEADS * t_new
    row = lambda b, c, pt: (b, 0)
    grid_spec = pltpu.PrefetchScalarGridSpec(
        num_scalar_prefetch=1,
        grid=(bd, n_pages // pps),
        in_specs=[pl.BlockSpec(memory_space=pl.ANY),
                  pl.BlockSpec((t_new, NSA_WIDTH), row), pl.BlockSpec((t_new, LANES), row),
                  pl.BlockSpec((t_new, NSA_WIDTH), row),
                  pl.BlockSpec((1, nbp, KVW), lambda b, c, pt: (b, 0, 0)),
                  pl.BlockSpec((t_new, KVW), row), pl.BlockSpec((t_new, KVW), row),
                  pl.BlockSpec((1, 1, KVW, w_buf), lambda b, c, pt: (layer, b, 0, 0))],
        out_specs=[pl.BlockSpec((t_new, NSA_WIDTH), row),
                   pl.BlockSpec((1, KVW, w_buf), lambda b, c, pt: (b, 0, 0))],
        scratch_shapes=[pltpu.VMEM((rows, LANES), F32), pltpu.VMEM((rows, nbp), BF),
                        pltpu.VMEM((rows, 1), F32), pltpu.VMEM((rows, 1), F32),
                        pltpu.VMEM((2, rows, LANES), F32), pltpu.VMEM((2 * nbp, pps * page_size), BF),
                        pltpu.VMEM((PAGE_SLOTS, pps, KVW, page_size), F32), pltpu.SemaphoreType.DMA((PAGE_SLOTS,))],
    )
    return pl.pallas_call(
        functools.partial(_nsa_sample_body, t_new=t_new, past_len=past_len, page_size=page_size, layer=layer),
        grid_spec=grid_spec,
        out_shape=[jax.ShapeDtypeStruct((bd * t_new, NSA_WIDTH), F32),
                   jax.ShapeDtypeStruct((bd, KVW, w_buf), F32)],
        compiler_params=_cparams(2),
        name="nsa_sample",
    )(page_table, cache4, q, misc, nz, cmp_past, kvs, kvw, win4)


def _gla_body(gla_ref, misc_ref, a2_ref, ab_ref, gn_ref, s0_ref, out_ref, sout_ref, st_ref, *, chunk, mxu_dtype):
    j = pl.program_id(1)
    n_steps = pl.num_programs(1)
    nseq, tb = gla_ref.shape[0], gla_ref.shape[1]
    sub = min(GLA_SUB, chunk)
    hk = GLA_HEADS * chunk
    n_c = tb // chunk
    probs = [(b, c) for b in range(nseq) for c in range(n_c)]

    diag_sv = _gla_diag(GLA_WIDTH, GLA_DV, GLA_KW, GLA_DK)
    diag_k = _gla_diag(hk, chunk, GLA_KW, GLA_DK)
    diag_v = _gla_diag(hk, chunk, GLA_WIDTH, GLA_DV)

    @pl.when(j == 0)
    def _():
        ri = lax.broadcasted_iota(jnp.int32, (GLA_WIDTH, GLA_DV), 0) % GLA_DV
        ci = lax.broadcasted_iota(jnp.int32, (GLA_WIDTH, GLA_DV), 1)
        pick = jnp.where(ri == ci, 1.0, 0.0).astype(BF)
        for b in range(nseq):
            s0 = s0_ref[b].reshape(GLA_KW, GLA_DV)
            wide = sum(_dot_nt(pick, part) for part in _split3(s0))
            st_ref[b] = jnp.where(diag_sv, wide, 0.0)

    tril = jnp.where(lax.broadcasted_iota(jnp.int32, (chunk, chunk), 0)
                     >= lax.broadcasted_iota(jnp.int32, (chunk, chunk), 1), 1.0, 0.0).astype(BF)
    srow = lax.broadcasted_iota(jnp.int32, (chunk, GLA_KW), 0)
    ri = lax.broadcasted_iota(jnp.int32, (GLA_WIDTH, GLA_WIDTH), 0) // GLA_DV
    ci = lax.broadcasted_iota(jnp.int32, (GLA_WIDTH, GLA_WIDTH), 1) // GLA_DV
    head_mean = jnp.where(ri == ci, 1.0 / GLA_DV, 0.0).astype(BF)
    a_col = lax.broadcasted_iota(jnp.int32, (sub, hk), 1) % chunk
    a_row = lax.broadcasted_iota(jnp.int32, (sub, hk), 0)

    def cast(a):
        return a.astype(mxu_dtype)

    def rows(ref, p):
        b, c = p
        return ref[b, c * chunk:(c + 1) * chunk, :]

    a2 = cast(a2_ref[...])
    a_logit = [_dot(cast(rows(misc_ref, p)), a2) + ab_ref[...] for p in probs]
    log_a = [(jnp.minimum(a, 0.0) - jnp.log(1.0 + jnp.exp(-jnp.abs(a)))) * (1.0 / GLA_TAU) for a in a_logit]
    la = [_split3(x) for x in log_a]
    bcum = [_dot(tril, l0) + _dot(tril, l1) + _dot(tril, l2) for (l0, l1, l2) in la]
    gq = [rows(gla_ref, p)[:, 0:GLA_KW] * (GLA_DK ** -0.5) for p in probs]
    gk = [rows(gla_ref, p)[:, GLA_KW:2 * GLA_KW] for p in probs]
    gv = [rows(gla_ref, p)[:, 2 * GLA_KW:2 * GLA_KW + GLA_WIDTH] for p in probs]
    qe = [cast(q * jnp.exp(b_)) for q, b_ in zip(gq, bcum)]
    v_bd = [cast(jnp.where(diag_v, jnp.concatenate([v] * GLA_HEADS, axis=0), 0.0)) for v in gv]
    intra = []
    for i, p in enumerate(probs):
        parts = []
        for sb in range(chunk // sub):
            t0 = sb * sub
            ref_row = bcum[i][t0:t0 + 1, :]
            qd = gq[i][t0:t0 + sub] * jnp.exp(bcum[i][t0:t0 + sub] - ref_row)
            kd = gk[i] * jnp.exp(jnp.where(srow < t0 + sub, ref_row - bcum[i], NEG))
            k_bd = jnp.where(diag_k, jnp.concatenate([kd] * GLA_HEADS, axis=0), 0.0)
            att = _dot_nt(cast(qd), cast(k_bd))
            att = jnp.where(a_col <= a_row + t0, att, 0.0)
            parts.append(_dot(cast(att), v_bd[i]))
        intra.append(jnp.concatenate(parts, axis=0))
    b_last = [b_[chunk - 1:chunk, :] for b_ in bcum]
    upd = [jnp.where(diag_sv, _dot(cast(_transpose_mxu(v, mxu_dtype)), cast(k * jnp.exp(bl - b_))), 0.0)
           for v, k, bl, b_ in zip(gv, gk, b_last, bcum)]
    decay = [jnp.exp(bl) for bl in b_last]
    o = []
    for b in range(nseq):
        st = st_ref[b]
        for c in range(n_c):
            i = b * n_c + c
            o.append(_dot_nt(qe[i], cast(st)) + intra[i])
            st = st * decay[i] + upd[i]
        st_ref[b] = st
    for i, (b, c) in enumerate(probs):
        q0, q1, _ = _split3(o[i] * o[i])
        ms = _dot(q0, head_mean) + _dot(q1, head_mean)
        gz = rows(gla_ref, (b, c))[:, 2 * GLA_KW + GLA_WIDTH:]
        out_ref[b, c * chunk:(c + 1) * chunk, :] = o[i] * lax.rsqrt(ms + EPS) * gn_ref[...] * _silu(gz)

    @pl.when(j == n_steps - 1)
    def _():
        for b in range(nseq):
            st = st_ref[b]
            acc = st[0:GLA_DV]
            for h in range(1, GLA_HEADS):
                acc = acc + st[h * GLA_DV:(h + 1) * GLA_DV]
            sout_ref[b] = _transpose_mxu(acc, F32).reshape(GLA_HEADS, GLA_DK, GLA_DV)


def _gla_diag(n_rows, row_group, n_cols, col_group):
    r = lax.broadcasted_iota(jnp.int32, (n_rows, n_cols), 0) // row_group
    c = lax.broadcasted_iota(jnp.int32, (n_rows, n_cols), 1) // col_group
    return r == c


def _gla(gla, misc, a2p, ab, gn, s0, *, batch, seq):
    chunk = min(GLA_CHUNK, seq)
    tb = min(GLA_TILE, seq)
    nseq = min(batch, GLA_PROBLEMS * chunk // tb)
    steps = seq // tb
    row = lambda g, j: (g, j, 0)
    fixed = lambda g, j: (0, 0)
    state = lambda g, j: (g, 0, 0, 0)
    o_gla, s_new = pl.pallas_call(
        functools.partial(_gla_body, chunk=chunk, mxu_dtype=BF if chunk >= 16 else F32),
        grid=(batch // nseq, steps),
        in_specs=[pl.BlockSpec((nseq, tb, W_GLA), row), pl.BlockSpec((nseq, tb, LANES), row),
                  pl.BlockSpec((LANES, GLA_KW), fixed), pl.BlockSpec((1, GLA_KW), fixed),
                  pl.BlockSpec((1, GLA_WIDTH), fixed),
                  pl.BlockSpec((nseq, GLA_HEADS, GLA_DK, GLA_DV), state)],
        out_specs=[pl.BlockSpec((nseq, tb, GLA_WIDTH), row),
                   pl.BlockSpec((nseq, GLA_HEADS, GLA_DK, GLA_DV), state)],
        out_shape=[jax.ShapeDtypeStruct((batch, seq, GLA_WIDTH), F32),
                   jax.ShapeDtypeStruct((batch, GLA_HEADS, GLA_DK, GLA_DV), F32)],
        scratch_shapes=[pltpu.VMEM((nseq, GLA_WIDTH, GLA_KW), F32)],
        compiler_params=_cparams(2),
        name="gla_prompt" if seq > GLA_CHUNK else "gla_sample",
    )(gla.reshape(batch, seq, W_GLA), misc.reshape(batch, seq, LANES), a2p, ab, gn, s0)
    return o_gla.reshape(batch * seq, GLA_WIDTH), s_new


def _out_body(x_ref, nsa_ref, gla_ref, conv_ref, halo_ref, buf_ref, cw_ref, wo_ref, g_ref, y_ref, nbuf_ref,
              *, seq, tm):
    cv = conv_ref[...]
    cb = cv[:, 0:CONV_WIDTH]
    u = cv[:, CONV_WIDTH:2 * CONV_WIDTH] * cv[:, 2 * CONV_WIDTH:3 * CONV_WIDTH]
    cz = cv[:, 3 * CONV_WIDTH:]
    rows = lax.broadcasted_iota(jnp.int32, (tm, CONV_WIDTH), 0)
    if seq >= tm:
        first = pl.program_id(0) % (seq // tm) == 0
        hv = halo_ref[...]
        hu = hv[:, CONV_WIDTH:2 * CONV_WIDTH] * hv[:, 2 * CONV_WIDTH:3 * CONV_WIDTH]
        bufv = buf_ref[0]
        p1 = jnp.where(first, bufv[1:2], hu[7:8])
        p2 = jnp.where(first, bufv[0:1], hu[6:7])
        prev1 = jnp.where(rows == 0, p1, pltpu.roll(u, 1, axis=0))
        prev2 = jnp.where(rows == 0, p2, jnp.where(rows == 1, p1, pltpu.roll(u, 2, axis=0)))
        nbuf_ref[0] = u[tm - (CONV_K - 1):, :]
    else:
        nseq = tm // seq
        bufv = buf_ref[...]
        b0 = jnp.broadcast_to(bufv[:, 0:1, :], (nseq, seq, CONV_WIDTH)).reshape(tm, CONV_WIDTH)
        b1 = jnp.broadcast_to(bufv[:, 1:2, :], (nseq, seq, CONV_WIDTH)).reshape(tm, CONV_WIDTH)
        t = rows % seq
        prev1 = jnp.where(t == 0, b1, pltpu.roll(u, 1, axis=0))
        prev2 = jnp.where(t == 0, b0, jnp.where(t == 1, b1, pltpu.roll(u, 2, axis=0)))
        nbuf_ref[...] = u.reshape(nseq, seq, CONV_WIDTH)[:, seq - (CONV_K - 1):, :]
    cw = cw_ref[...]
    y = prev2 * cw[0:1] + prev1 * cw[1:2] + u * cw[2:3]
    o_conv = cb * y * _silu(cz)
    acc = _dot(nsa_ref[...].astype(BF), wo_ref[0:NSA_WIDTH, :])
    acc = acc + _dot(gla_ref[...].astype(BF), wo_ref[NSA_WIDTH:NSA_WIDTH + GLA_WIDTH, :])
    acc = acc + _dot(o_conv.astype(BF), wo_ref[NSA_WIDTH + GLA_WIDTH:, :])
    normed = acc * lax.rsqrt(jnp.mean(acc * acc, axis=-1, keepdims=True) + EPS) * g_ref[...]
    y_ref[...] = x_ref[...] + normed


def _out(x2, o_nsa, o_gla, conv, buf, cw, wo, g, *, batch, seq):
    n, d = x2.shape
    tm = PROJ_TM if seq >= PROJ_TM else n
    steps = n // tm
    row = lambda i: (i, 0)
    fixed = lambda i: (0, 0)
    if seq >= tm:
        per_seq = seq // tm
        halo_spec = pl.BlockSpec((8, W_CONV), lambda i: (jnp.maximum(i * (tm // 8) - 1, 0), 0))
        buf_spec = pl.BlockSpec((1, CONV_K - 1, CONV_WIDTH), lambda i: (i // per_seq, 0, 0))
    else:
        halo_spec = pl.BlockSpec((8, W_CONV), fixed)
        buf_spec = pl.BlockSpec((batch, CONV_K - 1, CONV_WIDTH), lambda i: (0, 0, 0))
    return pl.pallas_call(
        functools.partial(_out_body, seq=seq, tm=tm),
        grid=(steps,),
        in_specs=[pl.BlockSpec((tm, d), row), pl.BlockSpec((tm, NSA_WIDTH), row),
                  pl.BlockSpec((tm, GLA_WIDTH), row), pl.BlockSpec((tm, W_CONV), row),
                  halo_spec, buf_spec,
                  pl.BlockSpec((CONV_K, CONV_WIDTH), fixed), pl.BlockSpec((d, d), fixed),
                  pl.BlockSpec((1, d), fixed)],
        out_specs=[pl.BlockSpec((tm, d), row), buf_spec],
        out_shape=[jax.ShapeDtypeStruct((n, d), F32),
                   jax.ShapeDtypeStruct((batch, CONV_K - 1, CONV_WIDTH), F32)],
        compiler_params=_cparams(1),
        name="out_prompt" if seq >= PROJ_TM else "out_sample",
    )(x2, o_nsa, o_gla, conv, conv, buf, cw, wo, g)


def _reorder_w_in(w):
    o_gate = NSA_WIDTH + 3 * KVW
    o_nz = o_gate + N_GATE
    o_gla = o_nz + NSA_WIDTH
    o_a = o_gla + 2 * GLA_KW + GLA_WIDTH
    o_gz = o_a + GLA_RANK
    o_conv = o_gz + GLA_WIDTH
    pad = jnp.zeros((w.shape[0], LANES - N_GATE - GLA_RANK), w.dtype)
    return jnp.concatenate([w[:, :o_gate], w[:, o_nz:o_gla], w[:, o_gla:o_a], w[:, o_gz:o_conv], w[:, o_conv:],
                            w[:, o_gate:o_nz], w[:, o_a:o_gz], pad], axis=1)


def _layer(x, l, P, *, prompt, cache_cmp_kv=None, cache_slc_kv=None, cache_win_kv=None, page_table=None,
           gla_s0=None, conv_buf=None):
    batch, seq, d = x.shape
    x2 = x.reshape(batch * seq, d)
    outs = _proj(x2, P['norm_pre'][l], P['w_in'][l], P['w_cmp'][l], batch=batch, seq=seq, prompt=prompt)
    q, kvc, kvs, kvw, nz, gla, conv, misc = outs[:8]
    kv_shape = (batch, seq, 2, NSA_KV, HEAD_DIM)
    if prompt:
        ksb, kwb, vst, vwt, cmp = outs[8:]
        o_nsa = _nsa_prompt(q, misc, nz, cmp, ksb, vst, kwb, vwt, batch=batch, seq=seq)
        w_keep = min(WINDOW, seq)
        new_w = kvw.reshape(kv_shape)[:, seq - w_keep:]
    else:
        cmp_past = _cmp_past(cache_cmp_kv, page_table, P['w_cmp'][l], layer=l)
        o_nsa, new_w = _nsa_sample(q, misc, nz, cmp_past, kvs, kvw, cache_slc_kv, cache_win_kv, page_table,
                                   layer=l, t_new=seq)
        w_buf = new_w.shape[2]
        new_w = jnp.transpose(new_w.reshape((batch,) + kv_shape[2:] + (w_buf,)), (0, 4, 1, 2, 3))
    o_gla, s_gla = _gla(gla, misc, P['gla_a2'][l], P['gla_ab'][l], P['gla_norm'][l], gla_s0, batch=batch, seq=seq)
    y2, new_buf = _out(x2, o_nsa, o_gla, conv, conv_buf, P['conv_w'][l], P['w_out'][l], P['norm_post'][l],
                       batch=batch, seq=seq)
    return y2.reshape(batch, seq, d), (kvc.reshape(kv_shape), kvs.reshape(kv_shape), new_w, s_gla, new_buf)


def kernel(x_prompt, x_sample, cache_cmp_kv, cache_slc_kv, cache_win_kv, state_gla, state_conv, page_table,
           norm_pre, norm_post, w_in, w_out, w_cmp_k, w_cmp_v, gla_a2, gla_ab, gla_norm, conv_w):
    depth = w_in.shape[0]
    d = w_in.shape[1]
    ones = jnp.ones((1, HALF), F32)
    a2p = jnp.zeros((depth, LANES, GLA_KW), F32).at[:, N_GATE:N_GATE + GLA_RANK, :].set(gla_a2)
    P = {
        'norm_pre': norm_pre.reshape(depth, 1, d),
        'norm_post': norm_post.reshape(depth, 1, d),
        'w_in': jax.vmap(_reorder_w_in)(w_in).astype(BF),
        'w_out': w_out.astype(BF),
        'w_cmp': jnp.concatenate([w_cmp_k[:, :, None] * ones, w_cmp_v[:, :, None] * ones], axis=2),
        'gla_a2': a2p,
        'gla_ab': gla_ab.reshape(depth, 1, GLA_KW),
        'gla_norm': jnp.tile(gla_norm, (1, GLA_HEADS)).reshape(depth, 1, GLA_WIDTH),
        'conv_w': conv_w,
    }
    xp, xs = x_prompt, x_sample
    bp = xp.shape[0]
    st_p, st_s = [], []
    cache_cmp_kv, cache_slc_kv, cache_win_kv = (_rows_minor(c) for c in (cache_cmp_kv, cache_slc_kv, cache_win_kv))
    for l in range(depth):
        xp, sp = _layer(xp, l, P, prompt=True,
                        gla_s0=jnp.zeros((bp, GLA_HEADS, GLA_DK, GLA_DV), F32),
                        conv_buf=jnp.zeros((bp, CONV_K - 1, CONV_WIDTH), F32))
        xs, ss = _layer(xs, l, P, prompt=False, cache_cmp_kv=cache_cmp_kv, cache_slc_kv=cache_slc_kv,
                        cache_win_kv=cache_win_kv, page_table=page_table,
                        gla_s0=state_gla[l], conv_buf=state_conv[l])
        st_p.append(sp)
        st_s.append(ss)
    stack = lambda sts, k: jnp.stack([s[k] for s in sts])
    return (xp, xs, stack(st_p, 0), stack(st_s, 0), stack(st_p, 1), stack(st_s, 1), stack(st_p, 2), stack(st_s, 2),
            stack(st_p, 3), stack(st_s, 3), stack(st_p, 4), stack(st_s, 4))
```

```python
import functools

import jax
import jax.numpy as jnp
from jax import lax
from jax.experimental import pallas as pl
from jax.experimental.pallas import tpu as pltpu

F32 = jnp.float32
BF = jnp.bfloat16

HEAD_DIM = 64
NSA_KV = 2
NSA_G = 4
NSA_HEADS = NSA_KV * NSA_G
NSA_WIDTH = NSA_HEADS * HEAD_DIM
KVW = 2 * NSA_KV * HEAD_DIM
HALF = NSA_KV * HEAD_DIM
CMP_BLOCK = 64
SEL_BLOCK = 64
TOP_K = 16
WINDOW = 512
GLA_HEADS = 4
GLA_DK = 32
GLA_DV = 64
GLA_KW = GLA_HEADS * GLA_DK
GLA_WIDTH = GLA_HEADS * GLA_DV
GLA_RANK = 16
GLA_TAU = 16.0
GLA_CHUNK = 64
GLA_SUB = 16
GLA_TILE = 256
GLA_PROBLEMS = 8
CONV_WIDTH = 256
CONV_K = 3
N_GATE = 3 * NSA_HEADS
EPS = 1e-6
NEG = -1e30
LOG2E = 1.4426950408889634

LANES = 128
VMEM_LIMIT = 48 * 1024 * 1024

C_Q = 0
C_CMP = C_Q + NSA_WIDTH
C_SLC = C_CMP + KVW
C_WIN = C_SLC + KVW
C_NZ = C_WIN + KVW
C_GLA = C_NZ + NSA_WIDTH
W_GLA = 2 * GLA_KW + 2 * GLA_WIDTH
C_CONV = C_GLA + W_GLA
W_CONV = 4 * CONV_WIDTH
C_MISC = C_CONV + W_CONV
PROJ_PAD = C_MISC + LANES

PROJ_TM = 512
NSA_TQ = 128
SEL_GROUP = 32
PAGES_PER_STEP = 32
PAGE_SLOTS = 3


def _cparams(n_axes, vmem=VMEM_LIMIT):
    return pltpu.CompilerParams(dimension_semantics=("arbitrary",) * n_axes, vmem_limit_bytes=vmem)


def _dot(a, b):
    return jnp.dot(a, b, preferred_element_type=F32)


def _dot_nt(a, b):
    return lax.dot_general(a, b, (((1,), (1,)), ((), ())), preferred_element_type=F32)


def _split3(a):
    a0 = a.astype(BF)
    r = a - a0.astype(F32)
    a1 = r.astype(BF)
    a2 = (r - a1.astype(F32)).astype(BF)
    return a0, a1, a2


def _transpose_mxu(y, dtype):
    n = y.shape[1]
    eye = jnp.where(lax.broadcasted_iota(jnp.int32, (n, n), 0) == lax.broadcasted_iota(jnp.int32, (n, n), 1),
                    1.0, 0.0).astype(BF)
    if dtype == BF:
        return _dot_nt(eye, y.astype(BF))
    return sum(_dot_nt(eye, part) for part in _split3(y))


def _silu(x):
    return x * (1.0 / (1.0 + jnp.exp(-x)))


def _sigmoid(x):
    return 1.0 / (1.0 + jnp.exp(-x))


def _proj_body(x_ref, g_ref, w_ref, wc_ref, *outs, prompt, seq):
    x = x_ref[...]
    h = x * lax.rsqrt(jnp.mean(x * x, axis=-1, keepdims=True) + EPS) * g_ref[...]
    hb = h.astype(BF)

    def proj(c0, n):
        return _dot(hb, w_ref[:, c0:c0 + n])

    q_ref, kvc_ref, kvs_ref, kvw_ref, nz_ref, gla_ref, conv_ref, misc_ref = outs[:8]
    q_ref[...] = proj(C_Q, NSA_WIDTH)
    kvc = proj(C_CMP, KVW)
    kvs = proj(C_SLC, KVW)
    kvw = proj(C_WIN, KVW)
    kvc_ref[...] = kvc
    kvs_ref[...] = kvs
    kvw_ref[...] = kvw
    nz_ref[...] = proj(C_NZ, NSA_WIDTH)
    gla_ref[...] = proj(C_GLA, W_GLA)
    conv_ref[...] = proj(C_CONV, W_CONV)
    misc_ref[...] = proj(C_MISC, LANES)
    if prompt:
        ksb_ref, kwb_ref, vst_ref, vwt_ref, cmp_ref = outs[8:]
        tm = x.shape[0]
        blk = (pl.program_id(0) % (seq // tm)) * (tm // SEL_BLOCK) \
            + lax.broadcasted_iota(jnp.int32, (tm, LANES), 0) // SEL_BLOCK
        onehot = jnp.where(lax.broadcasted_iota(jnp.int32, (tm, LANES), 1) == blk, 1.0, 0.0)
        ksb_ref[...] = jnp.concatenate([kvs[:, :HALF], onehot], axis=1).astype(BF)
        kwb_ref[...] = kvw[:, :HALF].astype(BF)
        vst_ref[0, 0] = kvs[:, HALF:].T.astype(BF)
        for j in range(tm // LANES):
            vwt_ref[0, j] = kvw[j * LANES:(j + 1) * LANES, HALF:].T.astype(BF)
        cmp_ref[...] = jnp.sum(kvc.reshape(tm // CMP_BLOCK, CMP_BLOCK, KVW) * wc_ref[...][None], axis=1)


def _proj(x2, g, w, wc, *, batch, seq, prompt):
    n, d = x2.shape
    tm = PROJ_TM if prompt else n
    steps = n // tm
    per_seq = seq // tm if prompt else 1
    row = lambda i: (i, 0)
    fixed = lambda i: (0, 0)
    widths = [NSA_WIDTH, KVW, KVW, KVW, NSA_WIDTH, W_GLA, W_CONV, LANES]
    out_shape = [jax.ShapeDtypeStruct((n, c), F32) for c in widths]
    out_specs = [pl.BlockSpec((tm, c), row) for c in widths]
    if prompt:
        out_shape += [jax.ShapeDtypeStruct((n, HALF + LANES), BF), jax.ShapeDtypeStruct((n, HALF), BF),
                      jax.ShapeDtypeStruct((batch, seq // tm, HALF, tm), BF),
                      jax.ShapeDtypeStruct((batch, seq // LANES, HALF, LANES), BF),
                      jax.ShapeDtypeStruct((n // CMP_BLOCK, KVW), F32)]
        out_specs += [pl.BlockSpec((tm, HALF + LANES), row), pl.BlockSpec((tm, HALF), row),
                      pl.BlockSpec((1, 1, HALF, tm), lambda i: (i // per_seq, i % per_seq, 0, 0)),
                      pl.BlockSpec((1, tm // LANES, HALF, LANES), lambda i: (i // per_seq, i % per_seq, 0, 0)),
                      pl.BlockSpec((tm // CMP_BLOCK, KVW), row)]
    return pl.pallas_call(
        functools.partial(_proj_body, prompt=prompt, seq=seq),
        grid=(steps,),
        in_specs=[pl.BlockSpec((tm, d), row), pl.BlockSpec((1, d), fixed),
                  pl.BlockSpec((d, PROJ_PAD), fixed), pl.BlockSpec((CMP_BLOCK, KVW), fixed)],
        out_specs=out_specs,
        out_shape=out_shape,
        compiler_params=_cparams(1),
        name="proj_prompt" if prompt else "proj_sample",
    )(x2, g, w, wc)


def _select_blocks(imp, forced, visible, n_pick):
    nb = imp.shape[0]
    blk = lax.broadcasted_iota(jnp.int32, imp.shape, 0)
    score = jnp.where(forced, -jnp.inf, jnp.where(visible, imp, -1.0))
    sel = forced
    for _ in range(n_pick):
        m = jnp.max(score, axis=0, keepdims=True)
        first = jnp.min(jnp.where(score == m, blk, nb), axis=0, keepdims=True)
        hit = blk == first
        sel = sel | hit
        score = jnp.where(hit, -jnp.inf, score)
    return sel


def _nsa_prompt_body(q_ref, misc_ref, nz_ref, cmp_ref, ks_ref, vst_ref, kw_ref, vwt_ref, out_ref, s_ref, *, seq):
    tq = NSA_TQ
    tk = PROJ_TM
    nb = seq // SEL_BLOCK
    n_chunks = seq // tk
    wl = NSA_G * tq
    n_piece = WINDOW // LANES + 1
    i = pl.program_id(1)
    q0 = i * tq
    qf = q_ref[...] * (HEAD_DIM ** -0.5 * LOG2E)
    gate_t = _sigmoid(misc_ref[...]).T
    lane_head = lax.broadcasted_iota(jnp.int32, (tq, LANES), 1) // HEAD_DIM
    kc = cmp_ref[0]
    kcb = kc[:, :HALF].astype(BF)
    vct = kc[:, HALF:].T.astype(BF)

    ql = lax.broadcasted_iota(jnp.int32, (1, wl), 1) % tq
    qpos = q0 + ql
    row_p = lax.broadcasted_iota(jnp.int32, (LANES, wl), 0)
    near_ok = row_p > ql
    diag_ok = row_p <= ql

    qs_h = []
    for h in range(NSA_KV):
        tiles = []
        for g in range(NSA_G):
            c = h * NSA_G + g
            t = qf[:, (c // 2) * LANES:(c // 2 + 1) * LANES]
            if c % 2 != h:
                t = pltpu.roll(t, HEAD_DIM, axis=1)
            tiles.append(jnp.where(lane_head == h, t, 0.0))
        qs_h.append(jnp.concatenate(tiles, axis=0).astype(BF))

    def compressed_and_choice(nbv):
        blk_w = lax.broadcasted_iota(jnp.int32, (nbv, wl), 0)
        vis_c = blk_w * CMP_BLOCK + (CMP_BLOCK - 1) <= qpos
        blk2 = lax.broadcasted_iota(jnp.int32, (nbv, NSA_KV * tq), 0)
        cur = (q0 + lax.broadcasted_iota(jnp.int32, (nbv, NSA_KV * tq), 1) % tq) // SEL_BLOCK
        forced = (blk2 == 0) | (blk2 == cur) | (blk2 == cur - 1)
        visible = blk2 <= cur
        oc, imps = [], []
        for h in range(NSA_KV):
            s = jnp.where(vis_c, _dot_nt(kcb[:nbv], qs_h[h]), NEG)
            m = jnp.max(s, axis=0, keepdims=True)
            e = jnp.where(vis_c, jnp.exp2(s - m), 0.0)
            l = jnp.sum(e, axis=0, keepdims=True)
            p_c = e / jnp.where(l > 0.0, l, 1.0)
            oc.append(_dot(vct[h * HEAD_DIM:(h + 1) * HEAD_DIM, :nbv], p_c.astype(BF)))
            imp = p_c[:, 0:tq]
            for g in range(1, NSA_G):
                imp = imp + p_c[:, g * tq:(g + 1) * tq]
            imps.append(imp)
        sel = _select_blocks(jnp.concatenate(imps, axis=1), forced, visible, TOP_K - 3)
        bias = jnp.where(sel, 0.0, NEG)
        if nbv < LANES:
            bias = jnp.concatenate([bias, jnp.full((LANES - nbv, NSA_KV * tq), NEG, F32)], axis=0)
        return oc[0], oc[1], bias

    sizes = list(range(SEL_GROUP, nb, SEL_GROUP)) + [nb]
    oc0, oc1, bias = lax.switch(jnp.minimum((2 * i + 1) // SEL_GROUP, len(sizes) - 1),
                                [functools.partial(compressed_and_choice, n) for n in sizes])
    oc_h = [oc0, oc1]
    qa_h = []
    for h in range(NSA_KV):
        bias_t = bias[:, h * tq:(h + 1) * tq].T.astype(BF)
        qa_h.append(jnp.concatenate([qs_h[h], jnp.concatenate([bias_t] * NSA_G, axis=0)], axis=1))

    sw_h = []
    for h in range(NSA_KV):
        s_parts = []
        for j in range(n_piece):
            start = q0 - WINDOW + j * LANES
            cl = pl.multiple_of(jnp.maximum(start, 0), LANES)
            sw = _dot_nt(kw_ref[pl.ds(cl, LANES), :], qs_h[h])
            if j == 0:
                sw = jnp.where(near_ok, sw, NEG)
            if j == n_piece - 1:
                sw = jnp.where(diag_ok, sw, NEG)
            else:
                sw = jnp.where(start >= 0, sw, NEG)
            s_parts.append(sw)
        sw_h.append(jnp.concatenate(s_parts, axis=0))

    def put_scores(c, slot):
        ka = ks_ref[pl.ds(pl.multiple_of(c * tk, tk), tk), :]
        for h in range(NSA_KV):
            s_ref[slot, h] = _dot_nt(ka, qa_h[h])

    def softmax_pv(c, slot, stats, causal):
        new, probs = [], []
        for h in range(NSA_KV):
            m_i, l_i, _ = stats[h]
            s = s_ref[slot, h]
            if causal:
                kpos = c * tk + lax.broadcasted_iota(jnp.int32, (tk, wl), 0)
                s = jnp.where(kpos <= qpos, s, NEG)
            m_n = jnp.maximum(m_i, jnp.max(s, axis=0, keepdims=True))
            alpha = jnp.exp2(m_i - m_n)
            p = jnp.exp2(s - m_n)
            new.append((m_n, alpha * l_i + jnp.sum(p, axis=0, keepdims=True), alpha))
            probs.append(p.astype(BF))
        out = []
        for h in range(NSA_KV):
            m_n, l_n, alpha = new[h]
            vt = vst_ref[0, c, h * HEAD_DIM:(h + 1) * HEAD_DIM, :]
            out.append((m_n, l_n, alpha * stats[h][2] + _dot(vt, probs[h])))
        return tuple(out)

    def pair(p, stats):
        put_scores(2 * p + 1, 1)
        stats = softmax_pv(2 * p, 0, stats, False)
        put_scores(2 * p + 2, 0)
        return softmax_pv(2 * p + 1, 1, stats, False)

    init = (jnp.full((1, wl), NEG, F32), jnp.zeros((1, wl), F32), jnp.zeros((HEAD_DIM, wl), F32))
    n_full = q0 // tk
    n_pair = n_full // 2
    put_scores(0, 0)
    stats = lax.fori_loop(0, n_pair, pair, (init,) * NSA_KV)
    put_scores(jnp.minimum(2 * n_pair + 1, n_chunks - 1), 1)
    stats = softmax_pv(2 * n_pair, 0, stats, True)
    carry = lax.cond(n_full % 2 == 1, lambda st: softmax_pv(n_full, 1, st, True), lambda st: st, stats)

    for h in range(NSA_KV):
        m_s, l_s, acc_s = carry[h]
        o_s = acc_s / l_s
        sw = sw_h[h]
        mw = jnp.max(sw, axis=0, keepdims=True)
        pw = jnp.exp2(sw - mw)
        lw = jnp.sum(pw, axis=0, keepdims=True)
        pwb = pw.astype(BF)
        o_w = jnp.zeros((HEAD_DIM, wl), F32)
        for j in range(n_piece):
            cj = jnp.maximum(i - WINDOW // LANES + j, 0)
            vt = vwt_ref[0, cj, h * HEAD_DIM:(h + 1) * HEAD_DIM, :]
            o_w = o_w + _dot(vt, pwb[j * LANES:(j + 1) * LANES])
        o_w = o_w / lw

        def gate_row(j):
            return jnp.concatenate(
                [gate_t[(h * NSA_G + g) * 3 + j:(h * NSA_G + g) * 3 + j + 1, :] for g in range(NSA_G)], axis=1)

        o_t = gate_row(0) * oc_h[h] + gate_row(1) * o_s + gate_row(2) * o_w
        for gp in range(NSA_G // 2):
            pr = jnp.concatenate([o_t[:, (2 * gp) * tq:(2 * gp + 1) * tq],
                                  o_t[:, (2 * gp + 1) * tq:(2 * gp + 2) * tq]], axis=0)
            c0 = (h * (NSA_G // 2) + gp) * LANES
            out_ref[:, c0:c0 + LANES] = pr.T * _silu(nz_ref[:, c0:c0 + LANES])


def _nsa_prompt(q, misc, nz, cmp, ksa, vst, kwb, vwt, *, batch, seq):
    tq = NSA_TQ
    nq = seq // tq
    nb = seq // SEL_BLOCK
    assert tq == LANES and WINDOW % LANES == 0 and nb <= LANES
    row = lambda b, i: (b * nq + i, 0)
    return pl.pallas_call(
        functools.partial(_nsa_prompt_body, seq=seq),
        grid=(batch, nq),
        in_specs=[pl.BlockSpec((tq, NSA_WIDTH), row), pl.BlockSpec((tq, LANES), row),
                  pl.BlockSpec((tq, NSA_WIDTH), row),
                  pl.BlockSpec((1, nb, KVW), lambda b, i: (b, 0, 0)),
                  pl.BlockSpec((seq, HALF + LANES), lambda b, i: (b, 0)),
                  pl.BlockSpec((1, seq // PROJ_TM, HALF, PROJ_TM), lambda b, i: (b, 0, 0, 0)),
                  pl.BlockSpec((seq, HALF), lambda b, i: (b, 0)),
                  pl.BlockSpec((1, seq // LANES, HALF, LANES), lambda b, i: (b, 0, 0, 0))],
        out_specs=pl.BlockSpec((tq, NSA_WIDTH), row),
        out_shape=jax.ShapeDtypeStruct((batch * seq, NSA_WIDTH), F32),
        scratch_shapes=[pltpu.VMEM((2, NSA_KV, PROJ_TM, NSA_G * tq), F32)],
        compiler_params=_cparams(2),
        name="nsa_prompt",
    )(q, misc, nz, cmp.reshape(batch, nb, KVW), ksa, vst, kwb, vwt)


def _page_copies(pt_ref, cache_ref, buf_ref, sem_ref, step, *, layer, n_chunks):
    pps = PAGES_PER_STEP
    slot = step % PAGE_SLOTS
    b = step // n_chunks
    c = step % n_chunks
    return [pltpu.make_async_copy(cache_ref.at[layer, pt_ref[b, c * pps + r]], buf_ref.at[slot, r], sem_ref.at[slot])
            for r in range(pps)]


def _fetch_pages(pt_ref, cache_ref, buf_ref, sem_ref, *, layer, n_chunks):
    step = pl.program_id(0) * n_chunks + pl.program_id(1)
    last = pl.num_programs(0) * n_chunks - 1
    ahead = PAGE_SLOTS - 1
    copies = functools.partial(_page_copies, pt_ref, cache_ref, buf_ref, sem_ref, layer=layer, n_chunks=n_chunks)

    @pl.when(step == 0)
    def _():
        for d in range(ahead):
            @pl.when(d <= last)
            def _():
                for cp in copies(step + d):
                    cp.start()

    @pl.when(step + ahead <= last)
    def _():
        for cp in copies(step + ahead):
            cp.start()

    for cp in copies(step):
        cp.wait()
    return buf_ref.at[step % PAGE_SLOTS]


def _cmp_past_body(pt_ref, cache_ref, wc_ref, out_ref, buf_ref, sem_ref, *, layer, n_chunks):
    pages = _fetch_pages(pt_ref, cache_ref, buf_ref, sem_ref, layer=layer, n_chunks=n_chunks)
    rows = jnp.concatenate([pages[r].T for r in range(PAGES_PER_STEP)], axis=0)
    blocks = rows.reshape(rows.shape[0] // CMP_BLOCK, CMP_BLOCK, KVW)
    out_ref[0] = jnp.sum(blocks * wc_ref[...][None], axis=1)


def _rows_minor(cache):
    n_layers, n_pool, n_rows = cache.shape[:3]
    return jnp.transpose(cache, (0, 1, 3, 4, 5, 2)).reshape(n_layers, n_pool, KVW, n_rows)


def _cmp_past(cache4, page_table, wc, *, layer):
    page_size = cache4.shape[3]
    bd, n_pages = page_table.shape
    pps = PAGES_PER_STEP
    per_page = page_size // CMP_BLOCK
    n_chunks = n_pages // pps
    grid_spec = pltpu.PrefetchScalarGridSpec(
        num_scalar_prefetch=1,
        grid=(bd, n_chunks),
        in_specs=[pl.BlockSpec(memory_space=pl.ANY), pl.BlockSpec((CMP_BLOCK, KVW), lambda b, c, pt: (0, 0))],
        out_specs=pl.BlockSpec((1, pps * per_page, KVW), lambda b, c, pt: (b, c, 0)),
        scratch_shapes=[pltpu.VMEM((PAGE_SLOTS, pps, KVW, page_size), F32), pltpu.SemaphoreType.DMA((PAGE_SLOTS,))],
    )
    return pl.pallas_call(
        functools.partial(_cmp_past_body, layer=layer, n_chunks=n_chunks),
        grid_spec=grid_spec,
        out_shape=jax.ShapeDtypeStruct((bd, n_pages * per_page, KVW), F32),
        compiler_params=_cparams(2),
        name="cmp_past",
    )(page_table, cache4, wc)


def _nsa_sample_body(pt_ref, cache_ref, q_ref, misc_ref, nz_ref, cmp_ref, kvs_ref, kvw_ref, win_ref, out_ref,
                     newwin_ref, qs_ref, masked_ref, m_ref, l_ref, acc_ref, expand_ref, buf_ref, sem_ref,
                     *, t_new, past_len, page_size, layer):
    pps = PAGES_PER_STEP
    c = pl.program_id(1)
    n_chunks = past_len // (pps * page_size)
    rows = NSA_HEADS * t_new
    nbp = past_len // SEL_BLOCK
    keys = pps * page_size
    bps = keys // SEL_BLOCK
    t_of_row = lax.broadcasted_iota(jnp.int32, (rows, 1), 0) % t_new
    eye = jnp.where(lax.broadcasted_iota(jnp.int32, (rows, rows), 0)
                    == lax.broadcasted_iota(jnp.int32, (rows, rows), 1), 1.0, 0.0).astype(BF)

    @pl.when((pl.program_id(0) == 0) & (c == 0))
    def _():
        eb = lax.broadcasted_iota(jnp.int32, (2 * nbp, keys), 0) - (nbp - bps)
        ek = lax.broadcasted_iota(jnp.int32, (2 * nbp, keys), 1) // SEL_BLOCK
        expand_ref[...] = jnp.where(eb == ek, NEG, 0.0).astype(BF)

    @pl.when(c == 0)
    def _():
        qf = q_ref[...] * (HEAD_DIM ** -0.5)
        lane_head = lax.broadcasted_iota(jnp.int32, (t_new, LANES), 1) // HEAD_DIM
        tiles = []
        for hg in range(NSA_HEADS):
            h = hg // NSA_G
            t = qf[:, (hg // 2) * LANES:(hg // 2 + 1) * LANES]
            if hg % 2 != h:
                t = pltpu.roll(t, HEAD_DIM, axis=1)
            tiles.append(jnp.where(lane_head == h, t, 0.0))
        qs = jnp.concatenate(tiles, axis=0)
        qs_ref[...] = qs
        kc = cmp_ref[0]
        s = _dot_nt(kc[:, :HALF].astype(BF), qs.astype(BF))
        m = jnp.max(s, axis=0, keepdims=True)
        e = jnp.exp(s - m)
        p_c = e / jnp.sum(e, axis=0, keepdims=True)
        p_ct = _dot_nt(eye, p_c.astype(BF)).astype(BF)
        acc_ref[1] = _dot(p_ct, kc[:, HALF:].astype(BF))
        ri = lax.broadcasted_iota(jnp.int32, (rows, rows), 0)
        ci = lax.broadcasted_iota(jnp.int32, (rows, rows), 1)
        same = ((ri // (NSA_G * t_new)) == (ci // (NSA_G * t_new))) & ((ri % t_new) == (ci % t_new))
        gsum = jnp.where(same, 1.0, 0.0).astype(BF)
        p0, p1, p2 = _split3(p_c)
        imp = _dot(p0, gsum) + _dot(p1, gsum) + _dot(p2, gsum)
        blk = lax.broadcasted_iota(jnp.int32, (nbp, rows), 0)
        forced = (blk == 0) | (blk == nbp - 1)
        sel = _select_blocks(imp, forced, blk >= 0, TOP_K - 3)
        masked_ref[...] = _dot_nt(eye, jnp.where(sel, 0.0, 1.0).astype(BF)).astype(BF)
        m_ref[...] = jnp.full(m_ref.shape, NEG, F32)
        l_ref[...] = jnp.zeros(l_ref.shape, F32)
        acc_ref[0] = jnp.zeros(acc_ref.shape[1:], F32)

    qsb = qs_ref[...].astype(BF)
    start = pl.multiple_of((nbp - bps) - c * bps, bps)
    bias = _dot(masked_ref[...], expand_ref[pl.ds(start, nbp), :])
    pages = _fetch_pages(pt_ref, cache_ref, buf_ref, sem_ref, layer=layer, n_chunks=n_chunks)
    kt = jnp.concatenate([pages[r, :HALF, :].astype(BF) for r in range(pps)], axis=1)
    vt = jnp.concatenate([pages[r, HALF:, :].astype(BF) for r in range(pps)], axis=1)
    s = _dot(qsb, kt) + bias
    m_i = m_ref[...]
    m_n = jnp.maximum(m_i, jnp.max(s, axis=1, keepdims=True))
    alpha = jnp.exp(m_i - m_n)
    p = jnp.exp(s - m_n)
    l_ref[...] = alpha * l_ref[...] + jnp.sum(p, axis=1, keepdims=True)
    acc_ref[0] = alpha * acc_ref[0] + _dot_nt(p.astype(BF), vt)
    m_ref[...] = m_n

    @pl.when(c == n_chunks - 1)
    def _():
        kvs = kvs_ref[...]
        tn = lax.broadcasted_iota(jnp.int32, (rows, t_new), 1)
        s_n = jnp.where(tn <= t_of_row, _dot_nt(qsb, kvs[:, :HALF].astype(BF)), NEG)
        m_i = m_ref[...]
        m_n = jnp.maximum(m_i, jnp.max(s_n, axis=1, keepdims=True))
        alpha = jnp.exp(m_i - m_n)
        p_n = jnp.exp(s_n - m_n)
        l_s = alpha * l_ref[...] + jnp.sum(p_n, axis=1, keepdims=True)
        o_s = (alpha * acc_ref[0] + _dot(p_n.astype(BF), kvs[:, HALF:].astype(BF))) / l_s
        wbt = win_ref[0, 0]
        w_buf = wbt.shape[1]
        kvw = kvw_ref[...]
        rb = lax.broadcasted_iota(jnp.int32, (rows, w_buf), 1)
        s_b = jnp.where(w_buf + t_of_row - rb < WINDOW, _dot(qsb, wbt[:HALF].astype(BF)), NEG)
        s_w = jnp.where(tn <= t_of_row, _dot_nt(qsb, kvw[:, :HALF].astype(BF)), NEG)
        m_w = jnp.maximum(jnp.max(s_b, axis=1, keepdims=True), jnp.max(s_w, axis=1, keepdims=True))
        p_b = jnp.exp(s_b - m_w)
        p_w = jnp.exp(s_w - m_w)
        l_w = jnp.sum(p_b, axis=1, keepdims=True) + jnp.sum(p_w, axis=1, keepdims=True)
        o_w = (_dot_nt(p_b.astype(BF), wbt[HALF:].astype(BF)) + _dot(p_w.astype(BF), kvw[:, HALF:].astype(BF))) / l_w
        shifted = pltpu.roll(wbt, w_buf - t_new, axis=1)
        newwin_ref[0, :, :w_buf - LANES] = shifted[:, :w_buf - LANES]
        tail = jnp.concatenate([jnp.zeros((LANES - t_new, KVW), F32), kvw], axis=0).T
        lane = lax.broadcasted_iota(jnp.int32, (KVW, LANES), 1)
        newwin_ref[0, :, w_buf - LANES:] = jnp.where(lane >= LANES - t_new, tail, shifted[:, w_buf - LANES:])
        gates = _sigmoid(misc_ref[...])
        o_c = acc_ref[1]
        nz = nz_ref[...]
        for hg in range(NSA_HEADS):
            h = hg // NSA_G
            r0 = hg * t_new
            o = (gates[:, hg * 3:hg * 3 + 1] * o_c[r0:r0 + t_new]
                 + gates[:, hg * 3 + 1:hg * 3 + 2] * o_s[r0:r0 + t_new]
                 + gates[:, hg * 3 + 2:hg * 3 + 3] * o_w[r0:r0 + t_new])
            c0 = hg * HEAD_DIM
            out_ref[:, c0:c0 + HEAD_DIM] = o[:, h * HEAD_DIM:(h + 1) * HEAD_DIM] * _silu(nz[:, c0:c0 + HEAD_DIM])


def _nsa_sample(q, misc, nz, cmp_past, kvs, kvw, cache4, win4, page_table, *, layer, t_new):
    page_size = cache4.shape[3]
    bd, n_pages = page_table.shape
    past_len = n_pages * page_size
    w_buf = win4.shape[3]
    pps = PAGES_PER_STEP
    nbp = past_len // SEL_BLOCK
    rows = NSA_HEADS * t_new
    row = lambda b, c, pt: (b, 0)
    grid_spec = pltpu.PrefetchScalarGridSpec(
        num_scalar_prefetch=1,
        grid=(bd, n_pages // pps),
        in_specs=[pl.BlockSpec(memory_space=pl.ANY),
                  pl.BlockSpec((t_new, NSA_WIDTH), row), pl.BlockSpec((t_new, LANES), row),
                  pl.BlockSpec((t_new, NSA_WIDTH), row),
                  pl.BlockSpec((1, nbp, KVW), lambda b, c, pt: (b, 0, 0)),
                  pl.BlockSpec((t_new, KVW), row), pl.BlockSpec((t_new, KVW), row),
                  pl.BlockSpec((1, 1, KVW, w_buf), lambda b, c, pt: (layer, b, 0, 0))],
        out_specs=[pl.BlockSpec((t_new, NSA_WIDTH), row),
                   pl.BlockSpec((1, KVW, w_buf), lambda b, c, pt: (b, 0, 0))],
        scratch_shapes=[pltpu.VMEM((rows, LANES), F32), pltpu.VMEM((rows, nbp), BF),
                        pltpu.VMEM((rows, 1), F32), pltpu.VMEM((rows, 1), F32),
                        pltpu.VMEM((2, rows, LANES), F32), pltpu.VMEM((2 * nbp, pps * page_size), BF),
                        pltpu.VMEM((PAGE_SLOTS, pps, KVW, page_size), F32), pltpu.SemaphoreType.DMA((PAGE_SLOTS,))],
    )
    return pl.pallas_call(
        functools.partial(_nsa_sample_body, t_new=t_new, past_len=past_len, page_size=page_size, layer=layer),
        grid_spec=grid_spec,
        out_shape=[jax.ShapeDtypeStruct((bd * t_new, NSA_WIDTH), F32),
                   jax.ShapeDtypeStruct((bd, KVW, w_buf), F32)],
        compiler_params=_cparams(2),
        name="nsa_sample",
    )(page_table, cache4, q, misc, nz, cmp_past, kvs, kvw, win4)


def _gla_body(gla_ref, misc_ref, a2_ref, ab_ref, gn_ref, s0_ref, out_ref, sout_ref, st_ref, *, chunk, mxu_dtype):
    j = pl.program_id(1)
    n_steps = pl.num_programs(1)
    nseq, tb = gla_ref.shape[0], gla_ref.shape[1]
    sub = min(GLA_SUB, chunk)
    hk = GLA_HEADS * chunk
    n_c = tb // chunk
    probs = [(b, c) for b in range(nseq) for c in range(n_c)]

    diag_sv = _gla_diag(GLA_WIDTH, GLA_DV, GLA_KW, GLA_DK)
    diag_k = _gla_diag(hk, chunk, GLA_KW, GLA_DK)
    diag_v = _gla_diag(hk, chunk, GLA_WIDTH, GLA_DV)

    @pl.when(j == 0)
    def _():
        ri = lax.broadcasted_iota(jnp.int32, (GLA_WIDTH, GLA_DV), 0) % GLA_DV
        ci = lax.broadcasted_iota(jnp.int32, (GLA_WIDTH, GLA_DV), 1)
        pick = jnp.where(ri == ci, 1.0, 0.0).astype(BF)
        for b in range(nseq):
            s0 = s0_ref[b].reshape(GLA_KW, GLA_DV)
            wide = sum(_dot_nt(pick, part) for part in _split3(s0))
            st_ref[b] = jnp.where(diag_sv, wide, 0.0)

    tril = jnp.where(lax.broadcasted_iota(jnp.int32, (chunk, chunk), 0)
                     >= lax.broadcasted_iota(jnp.int32, (chunk, chunk), 1), 1.0, 0.0).astype(BF)
    srow = lax.broadcasted_iota(jnp.int32, (chunk, GLA_KW), 0)
    ri = lax.broadcasted_iota(jnp.int32, (GLA_WIDTH, GLA_WIDTH), 0) // GLA_DV
    ci = lax.broadcasted_iota(jnp.int32, (GLA_WIDTH, GLA_WIDTH), 1) // GLA_DV
    head_mean = jnp.where(ri == ci, 1.0 / GLA_DV, 0.0).astype(BF)
    a_col = lax.broadcasted_iota(jnp.int32, (sub, hk), 1) % chunk
    a_row = lax.broadcasted_iota(jnp.int32, (sub, hk), 0)

    def cast(a):
        return a.astype(mxu_dtype)

    def rows(ref, p):
        b, c = p
        return ref[b, c * chunk:(c + 1) * chunk, :]

    a2 = cast(a2_ref[...])
    a_logit = [_dot(cast(rows(misc_ref, p)), a2) + ab_ref[...] for p in probs]
    log_a = [(jnp.minimum(a, 0.0) - jnp.log(1.0 + jnp.exp(-jnp.abs(a)))) * (1.0 / GLA_TAU) for a in a_logit]
    la = [_split3(x) for x in log_a]
    bcum = [_dot(tril, l0) + _dot(tril, l1) + _dot(tril, l2) for (l0, l1, l2) in la]
    gq = [rows(gla_ref, p)[:, 0:GLA_KW] * (GLA_DK ** -0.5) for p in probs]
    gk = [rows(gla_ref, p)[:, GLA_KW:2 * GLA_KW] for p in probs]
    gv = [rows(gla_ref, p)[:, 2 * GLA_KW:2 * GLA_KW + GLA_WIDTH] for p in probs]
    qe = [cast(q * jnp.exp(b_)) for q, b_ in zip(gq, bcum)]
    v_bd = [cast(jnp.where(diag_v, jnp.concatenate([v] * GLA_HEADS, axis=0), 0.0)) for v in gv]
    intra = []
    for i, p in enumerate(probs):
        parts = []
        for sb in range(chunk // sub):
            t0 = sb * sub
            ref_row = bcum[i][t0:t0 + 1, :]
            qd = gq[i][t0:t0 + sub] * jnp.exp(bcum[i][t0:t0 + sub] - ref_row)
            kd = gk[i] * jnp.exp(jnp.where(srow < t0 + sub, ref_row - bcum[i], NEG))
            k_bd = jnp.where(diag_k, jnp.concatenate([kd] * GLA_HEADS, axis=0), 0.0)
            att = _dot_nt(cast(qd), cast(k_bd))
            att = jnp.where(a_col <= a_row + t0, att, 0.0)
            parts.append(_dot(cast(att), v_bd[i]))
        intra.append(jnp.concatenate(parts, axis=0))
    b_last = [b_[chunk - 1:chunk, :] for b_ in bcum]
    upd = [jnp.where(diag_sv, _dot(cast(_transpose_mxu(v, mxu_dtype)), cast(k * jnp.exp(bl - b_))), 0.0)
           for v, k, bl, b_ in zip(gv, gk, b_last, bcum)]
    decay = [jnp.exp(bl) for bl in b_last]
    o = []
    for b in range(nseq):
        st = st_ref[b]
        for c in range(n_c):
            i = b * n_c + c
            o.append(_dot_nt(qe[i], cast(st)) + intra[i])
            st = st * decay[i] + upd[i]
        st_ref[b] = st
    for i, (b, c) in enumerate(probs):
        q0, q1, _ = _split3(o[i] * o[i])
        ms = _dot(q0, head_mean) + _dot(q1, head_mean)
        gz = rows(gla_ref, (b, c))[:, 2 * GLA_KW + GLA_WIDTH:]
        out_ref[b, c * chunk:(c + 1) * chunk, :] = o[i] * lax.rsqrt(ms + EPS) * gn_ref[...] * _silu(gz)

    @pl.when(j == n_steps - 1)
    def _():
        for b in range(nseq):
            st = st_ref[b]
            acc = st[0:GLA_DV]
            for h in range(1, GLA_HEADS):
                acc = acc + st[h * GLA_DV:(h + 1) * GLA_DV]
            sout_ref[b] = _transpose_mxu(acc, F32).reshape(GLA_HEADS, GLA_DK, GLA_DV)


def _gla_diag(n_rows, row_group, n_cols, col_group):
    r = lax.broadcasted_iota(jnp.int32, (n_rows, n_cols), 0) // row_group
    c = lax.broadcasted_iota(jnp.int32, (n_rows, n_cols), 1) // col_group
    return r == c


def _gla(gla, misc, a2p, ab, gn, s0, *, batch, seq):
    chunk = min(GLA_CHUNK, seq)
    tb = min(GLA_TILE, seq)
    nseq = min(batch, GLA_PROBLEMS * chunk // tb)
    steps = seq // tb
    row = lambda g, j: (g, j, 0)
    fixed = lambda g, j: (0, 0)
    state = lambda g, j: (g, 0, 0, 0)
    o_gla, s_new = pl.pallas_call(
        functools.partial(_gla_body, chunk=chunk, mxu_dtype=BF if chunk >= 16 else F32),
        grid=(batch // nseq, steps),
        in_specs=[pl.BlockSpec((nseq, tb, W_GLA), row), pl.BlockSpec((nseq, tb, LANES), row),
                  pl.BlockSpec((LANES, GLA_KW), fixed), pl.BlockSpec((1, GLA_KW), fixed),
                  pl.BlockSpec((1, GLA_WIDTH), fixed),
                  pl.BlockSpec((nseq, GLA_HEADS, GLA_DK, GLA_DV), state)],
        out_specs=[pl.BlockSpec((nseq, tb, GLA_WIDTH), row),
                   pl.BlockSpec((nseq, GLA_HEADS, GLA_DK, GLA_DV), state)],
        out_shape=[jax.ShapeDtypeStruct((batch, seq, GLA_WIDTH), F32),
                   jax.ShapeDtypeStruct((batch, GLA_HEADS, GLA_DK, GLA_DV), F32)],
        scratch_shapes=[pltpu.VMEM((nseq, GLA_WIDTH, GLA_KW), F32)],
        compiler_params=_cparams(2),
        name="gla_prompt" if seq > GLA_CHUNK else "gla_sample",
    )(gla.reshape(batch, seq, W_GLA), misc.reshape(batch, seq, LANES), a2p, ab, gn, s0)
    return o_gla.reshape(batch * seq, GLA_WIDTH), s_new


def _out_body(x_ref, nsa_ref, gla_ref, conv_ref, halo_ref, buf_ref, cw_ref, wo_ref, g_ref, y_ref, nbuf_ref,
              *, seq, tm):
    cv = conv_ref[...]
    cb = cv[:, 0:CONV_WIDTH]
    u = cv[:, CONV_WIDTH:2 * CONV_WIDTH] * cv[:, 2 * CONV_WIDTH:3 * CONV_WIDTH]
    cz = cv[:, 3 * CONV_WIDTH:]
    rows = lax.broadcasted_iota(jnp.int32, (tm, CONV_WIDTH), 0)
    if seq >= tm:
        first = pl.program_id(0) % (seq // tm) == 0
        hv = halo_ref[...]
        hu = hv[:, CONV_WIDTH:2 * CONV_WIDTH] * hv[:, 2 * CONV_WIDTH:3 * CONV_WIDTH]
        bufv = buf_ref[0]
        p1 = jnp.where(first, bufv[1:2], hu[7:8])
        p2 = jnp.where(first, bufv[0:1], hu[6:7])
        prev1 = jnp.where(rows == 0, p1, pltpu.roll(u, 1, axis=0))
        prev2 = jnp.where(rows == 0, p2, jnp.where(rows == 1, p1, pltpu.roll(u, 2, axis=0)))
        nbuf_ref[0] = u[tm - (CONV_K - 1):, :]
    else:
        nseq = tm // seq
        bufv = buf_ref[...]
        b0 = jnp.broadcast_to(bufv[:, 0:1, :], (nseq, seq, CONV_WIDTH)).reshape(tm, CONV_WIDTH)
        b1 = jnp.broadcast_to(bufv[:, 1:2, :], (nseq, seq, CONV_WIDTH)).reshape(tm, CONV_WIDTH)
        t = rows % seq
        prev1 = jnp.where(t == 0, b1, pltpu.roll(u, 1, axis=0))
        prev2 = jnp.where(t == 0, b0, jnp.where(t == 1, b1, pltpu.roll(u, 2, axis=0)))
        nbuf_ref[...] = u.reshape(nseq, seq, CONV_WIDTH)[:, seq - (CONV_K - 1):, :]
    cw = cw_ref[...]
    y = prev2 * cw[0:1] + prev1 * cw[1:2] + u * cw[2:3]
    o_conv = cb * y * _silu(cz)
    acc = _dot(nsa_ref[...].astype(BF), wo_ref[0:NSA_WIDTH, :])
    acc = acc + _dot(gla_ref[...].astype(BF), wo_ref[NSA_WIDTH:NSA_WIDTH + GLA_WIDTH, :])
    acc = acc + _dot(o_conv.astype(BF), wo_ref[NSA_WIDTH + GLA_WIDTH:, :])
    normed = acc * lax.rsqrt(jnp.mean(acc * acc, axis=-1, keepdims=True) + EPS) * g_ref[...]
    y_ref[...] = x_ref[...] + normed


def _out(x2, o_nsa, o_gla, conv, buf, cw, wo, g, *, batch, seq):
    n, d = x2.shape
    tm = PROJ_TM if seq >= PROJ_TM else n
    steps = n // tm
    row = lambda i: (i, 0)
    fixed = lambda i: (0, 0)
    if seq >= tm:
        per_seq = seq // tm
        halo_spec = pl.BlockSpec((8, W_CONV), lambda i: (jnp.maximum(i * (tm // 8) - 1, 0), 0))
        buf_spec = pl.BlockSpec((1, CONV_K - 1, CONV_WIDTH), lambda i: (i // per_seq, 0, 0))
    else:
        halo_spec = pl.BlockSpec((8, W_CONV), fixed)
        buf_spec = pl.BlockSpec((batch, CONV_K - 1, CONV_WIDTH), lambda i: (0, 0, 0))
    return pl.pallas_call(
        functools.partial(_out_body, seq=seq, tm=tm),
        grid=(steps,),
        in_specs=[pl.BlockSpec((tm, d), row), pl.BlockSpec((tm, NSA_WIDTH), row),
                  pl.BlockSpec((tm, GLA_WIDTH), row), pl.BlockSpec((tm, W_CONV), row),
                  halo_spec, buf_spec,
                  pl.BlockSpec((CONV_K, CONV_WIDTH), fixed), pl.BlockSpec((d, d), fixed),
                  pl.BlockSpec((1, d), fixed)],
        out_specs=[pl.BlockSpec((tm, d), row), buf_spec],
        out_shape=[jax.ShapeDtypeStruct((n, d), F32),
                   jax.ShapeDtypeStruct((batch, CONV_K - 1, CONV_WIDTH), F32)],
        compiler_params=_cparams(1),
        name="out_prompt" if seq >= PROJ_TM else "out_sample",
    )(x2, o_nsa, o_gla, conv, conv, buf, cw, wo, g)


def _reorder_w_in(w):
    o_gate = NSA_WIDTH + 3 * KVW
    o_nz = o_gate + N_GATE
    o_gla = o_nz + NSA_WIDTH
    o_a = o_gla + 2 * GLA_KW + GLA_WIDTH
    o_gz = o_a + GLA_RANK
    o_conv = o_gz + GLA_WIDTH
    pad = jnp.zeros((w.shape[0], LANES - N_GATE - GLA_RANK), w.dtype)
    return jnp.concatenate([w[:, :o_gate], w[:, o_nz:o_gla], w[:, o_gla:o_a], w[:, o_gz:o_conv], w[:, o_conv:],
                            w[:, o_gate:o_nz], w[:, o_a:o_gz], pad], axis=1)


def _layer(x, l, P, *, prompt, cache_cmp_kv=None, cache_slc_kv=None, cache_win_kv=None, page_table=None,
           gla_s0=None, conv_buf=None):
    batch, seq, d = x.shape
    x2 = x.reshape(batch * seq, d)
    outs = _proj(x2, P['norm_pre'][l], P['w_in'][l], P['w_cmp'][l], batch=batch, seq=seq, prompt=prompt)
    q, kvc, kvs, kvw, nz, gla, conv, misc = outs[:8]
    kv_shape = (batch, seq, 2, NSA_KV, HEAD_DIM)
    if prompt:
        ksb, kwb, vst, vwt, cmp = outs[8:]
        o_nsa = _nsa_prompt(q, misc, nz, cmp, ksb, vst, kwb, vwt, batch=batch, seq=seq)
        w_keep = min(WINDOW, seq)
        new_w = kvw.reshape(kv_shape)[:, seq - w_keep:]
    else:
        cmp_past = _cmp_past(cache_cmp_kv, page_table, P['w_cmp'][l], layer=l)
        o_nsa, new_w = _nsa_sample(q, misc, nz, cmp_past, kvs, kvw, cache_slc_kv, cache_win_kv, page_table,
                                   layer=l, t_new=seq)
        w_buf = new_w.shape[2]
        new_w = jnp.transpose(new_w.reshape((batch,) + kv_shape[2:] + (w_buf,)), (0, 4, 1, 2, 3))
    o_gla, s_gla = _gla(gla, misc, P['gla_a2'][l], P['gla_ab'][l], P['gla_norm'][l], gla_s0, batch=batch, seq=seq)
    y2, new_buf = _out(x2, o_nsa, o_gla, conv, conv_buf, P['conv_w'][l], P['w_out'][l], P['norm_post'][l],
                       batch=batch, seq=seq)
    return y2.reshape(batch, seq, d), (kvc.reshape(kv_shape), kvs.reshape(kv_shape), new_w, s_gla, new_buf)


def kernel(x_prompt, x_sample, cache_cmp_kv, cache_slc_kv, cache_win_kv, state_gla, state_conv, page_table,
           norm_pre, norm_post, w_in, w_out, w_cmp_k, w_cmp_v, gla_a2, gla_ab, gla_norm, conv_w):
    depth = w_in.shape[0]
    d = w_in.shape[1]
    ones = jnp.ones((1, HALF), F32)
    a2p = jnp.zeros((depth, LANES, GLA_KW), F32).at[:, N_GATE:N_GATE + GLA_RANK, :].set(gla_a2)
    P = {
        'norm_pre': norm_pre.reshape(depth, 1, d),
        'norm_post': norm_post.reshape(depth, 1, d),
        'w_in': jax.vmap(_reorder_w_in)(w_in).astype(BF),
        'w_out': w_out.astype(BF),
        'w_cmp': jnp.concatenate([w_cmp_k[:, :, None] * ones, w_cmp_v[:, :, None] * ones], axis=2),
        'gla_a2': a2p,
        'gla_ab': gla_ab.reshape(depth, 1, GLA_KW),
        'gla_norm': jnp.tile(gla_norm, (1, GLA_HEADS)).reshape(depth, 1, GLA_WIDTH),
        'conv_w': conv_w,
    }
    xp, xs = x_prompt, x_sample
    bp = xp.shape[0]
    st_p, st_s = [], []
    cache_cmp_kv, cache_slc_kv, cache_win_kv = (_rows_minor(c) for c in (cache_cmp_kv, cache_slc_kv, cache_win_kv))
    for l in range(depth):
        xp, sp = _layer(xp, l, P, prompt=True,
                        gla_s0=jnp.zeros((bp, GLA_HEADS, GLA_DK, GLA_DV), F32),
                        conv_buf=jnp.zeros((bp, CONV_K - 1, CONV_WIDTH), F32))
        xs, ss = _layer(xs, l, P, prompt=False, cache_cmp_kv=cache_cmp_kv, cache_slc_kv=cache_slc_kv,
                        cache_win_kv=cache_win_kv, page_table=page_table,
                        gla_s0=state_gla[l], conv_buf=state_conv[l])
        st_p.append(sp)
        st_s.append(ss)
    stack = lambda sts, k: jnp.stack([s[k] for s in sts])
    return (xp, xs, stack(st_p, 0), stack(st_s, 0), stack(st_p, 1), stack(st_s, 1), stack(st_p, 2), stack(st_s, 2),
            stack(st_p, 3), stack(st_s, 3), stack(st_p, 4), stack(st_s, 4))
```
